```python
import math
import jax, jax.numpy as jnp
from jax import lax
import numpy as np

D_MODEL = 1024
BATCH = 16
SEQ = 2048
DEPTH = 1
DEC_BATCH = 32
DEC_SEQ = 8
PAST_LEN = 16384
PAGE_SIZE = 128

CHUNK = 128
E_A = 1024
G_A = 8
C_A = E_A // G_A
H_B = 8
DH = 64
DV = 2 * DH
E_B = H_B * DV
P_DIM = 256
NUM_BUCKETS = 32
MAX_DISTANCE = 128
Q_BLOCK = 128
LN_EPS = 1e-5
RMS_EPS = 1e-5
ATTN_SCALE = DH ** -0.5
ALPHA = (2.0 * DEPTH) ** 0.25
BETA = (8.0 * DEPTH) ** -0.25
NEG_INF = -1e30
IN_COLS = 3 * E_A + 4 * E_B + 2 * D_MODEL

kernel_name = 'diff_gmlp_hybrid_step'


def _lambda_init(layer):
    return 0.8 - 0.6 * math.exp(-0.3 * layer)


def _layer_norm(x, g, b):
    xf = x.astype(jnp.float32)
    xc = xf - jnp.mean(xf, -1, keepdims=True)
    var = jnp.mean(xc * xc, -1, keepdims=True)
    return (xc * lax.rsqrt(var + LN_EPS) * g.astype(jnp.float32) + b.astype(jnp.float32)).astype(x.dtype)


def _rel_bias(table, q_pos, k_pos):
    n = jnp.maximum(q_pos[:, None] - k_pos[None, :], 0)
    max_exact = NUM_BUCKETS // 2
    nf = jnp.maximum(n, 1).astype(jnp.float32)
    large = max_exact + (jnp.log(nf / max_exact) / math.log(MAX_DISTANCE / max_exact)
                         * (NUM_BUCKETS - max_exact)).astype(jnp.int32)
    bucket = jnp.where(n < max_exact, n, jnp.minimum(large, NUM_BUCKETS - 1))
    return jnp.transpose(table[bucket].astype(jnp.float32), (2, 0, 1))


def _diff_partial(q, k, v, bias, mask):
    s = jnp.einsum('bqhcd,bkhcd->bhcqk', q.astype(jnp.float32), k.astype(jnp.float32)) * ATTN_SCALE
    s = s + bias[None, :, None]
    if mask is not None:
        s = jnp.where(mask, s, NEG_INF)
    m = jnp.max(s, -1)
    p = jnp.exp(s - m[..., None])
    return m, jnp.sum(p, -1), jnp.einsum('bhcqk,bkhv->bhcqv', p, v.astype(jnp.float32))


def _combine(a, b):
    m_a, l_a, acc_a = a
    m_b, l_b, acc_b = b
    m = jnp.maximum(m_a, m_b)
    ea = jnp.exp(m_a - m)
    eb = jnp.exp(m_b - m)
    return m, l_a * ea + l_b * eb, acc_a * ea[..., None] + acc_b * eb[..., None]


def _diff_finalize(l, acc, lam, lam_init, subln_g, dtype):
    attn = acc / l[..., None]
    o = attn[:, :, 0] - lam * attn[:, :, 1]
    o = o * lax.rsqrt(jnp.mean(o * o, -1, keepdims=True) + RMS_EPS) * subln_g.astype(jnp.float32)
    o = o * (1.0 - lam_init)
    B, H, T, _ = o.shape
    return jnp.transpose(o, (0, 2, 1, 3)).reshape(B, T, H * DV).astype(dtype)


def _diff_attn_prompt(q, k, v, table, lam, lam_init, subln_g):
    B, T = q.shape[:2]
    k_pos = jnp.arange(T)

    def block(i):
        start = i * Q_BLOCK
        qb = lax.dynamic_slice_in_dim(q, start, Q_BLOCK, axis=1)
        q_pos = start + jnp.arange(Q_BLOCK)
        _, l, acc = _diff_partial(qb, k, v, _rel_bias(table, q_pos, k_pos),
                                  q_pos[:, None] >= k_pos[None, :])
        return _diff_finalize(l, acc, lam, lam_init, subln_g, q.dtype)

    out = lax.map(block, jnp.arange(T // Q_BLOCK))
    return jnp.transpose(out, (1, 0, 2, 3)).reshape(B, T, E_B)


def _diff_attn_sample(q, k, v, cache_k, cache_v, page_table, layer, table, lam, lam_init, subln_g):
    B, T = q.shape[:2]
    n_pages = page_table.shape[1]
    q_pos = n_pages * PAGE_SIZE + jnp.arange(T)
    init = _diff_partial(q, k, v, _rel_bias(table, q_pos, q_pos), q_pos[:, None] >= q_pos[None, :])

    def step(carry, j):
        phys = page_table[:, j]
        kb = cache_k[layer, phys].reshape(B, PAGE_SIZE, H_B, 2, DH)
        vb = cache_v[layer, phys]
        k_pos = j * PAGE_SIZE + jnp.arange(PAGE_SIZE)
        return _combine(carry, _diff_partial(q, kb, vb, _rel_bias(table, q_pos, k_pos), None)), None

    (_, l, acc), _ = lax.scan(step, init, jnp.arange(n_pages))
    return _diff_finalize(l, acc, lam, lam_init, subln_g, q.dtype)


def _gmlp_branch(u, v, z, ln_g, ln_b, w_s, b_s, t_chunk):
    B, T, _ = v.shape
    u = jax.nn.gelu(u, approximate=False)
    v = _layer_norm(jax.nn.gelu(v, approximate=False), ln_g, ln_b)
    vc = v.reshape(B, T // t_chunk, t_chunk, G_A, C_A)
    tri = jnp.tril(jnp.ones((t_chunk, t_chunk), dtype=bool))
    ws = jnp.where(tri, w_s[:, :t_chunk, :t_chunk], 0.0)
    mixed = jnp.einsum('gts,bnsgc->bntgc', ws, vc) + jnp.transpose(b_s[:, :t_chunk])[None, None, :, :, None]
    return u * mixed.reshape(B, T, E_A) * jax.nn.silu(z), v


def _layer(x, p, attend, t_chunk, w_in, b_gate, gmlp_ln_g, gmlp_ln_b, w_s, b_s,
           w_pa, w_pb, w_o, ln_g, ln_b, w_pe, w_pg, b_pg):
    B, T, _ = x.shape
    h = x @ w_in
    u_a, v_a, z_a, q_b, k_b, v_b, z_b, g = jnp.split(
        h, [E_A, 2 * E_A, 3 * E_A, 3 * E_A + E_B, 3 * E_A + 2 * E_B, 3 * E_A + 3 * E_B,
            3 * E_A + 4 * E_B], axis=-1)
    g = jax.nn.sigmoid(g + b_gate)
    out_a, v_rows = _gmlp_branch(u_a, v_a, z_a, gmlp_ln_g, gmlp_ln_b, w_s, b_s, t_chunk)
    q = q_b.reshape(B, T, H_B, 2, DH)
    k = k_b.reshape(B, T, H_B, 2, DH)
    v = v_b.reshape(B, T, H_B, DV)
    out_b = attend(q, k, v) * jax.nn.silu(z_b)
    merged = g[..., :D_MODEL] * (out_a @ w_pa) + g[..., D_MODEL:] * (out_b @ w_pb)
    x1 = _layer_norm(ALPHA * x + merged @ w_o, ln_g, ln_b)
    y = x1 + jax.nn.sigmoid(x1 @ w_pg + b_pg) * (p @ w_pe)
    return y, k_b.reshape(B, T, H_B, 2 * DH), v, v_rows


def setup_inputs(seed: int = 0) -> dict:
    key = jax.random.key(seed)
    ks = jax.random.split(key, 26)
    f32 = jnp.float32
    n_pages = PAST_LEN // PAGE_SIZE
    n_used = DEC_BATCH * n_pages
    n_pool = (n_used * 5) // 4

    def nrm(k, shape, scale):
        return jax.random.normal(k, shape, f32) * scale

    page_table = jax.random.permutation(ks[6], n_pool)[:n_used].reshape(DEC_BATCH, n_pages).astype(jnp.int32)
    return {
        'x_prompt': nrm(ks[0], (BATCH, SEQ, D_MODEL), 1.0),
        'x_sample': nrm(ks[1], (DEC_BATCH, DEC_SEQ, D_MODEL), 1.0),
        'p_prompt': nrm(ks[2], (DEPTH, BATCH, SEQ, P_DIM), 1.0),
        'p_sample': nrm(ks[3], (DEPTH, DEC_BATCH, DEC_SEQ, P_DIM), 1.0),
        'cache_k': nrm(ks[4], (DEPTH, n_pool, PAGE_SIZE, H_B, 2 * DH), 1.0),
        'cache_v': nrm(ks[5], (DEPTH, n_pool, PAGE_SIZE, H_B, DV), 1.0),
        'page_table': page_table,
        'rel_table': nrm(ks[7], (NUM_BUCKETS, H_B), 0.5),
        'w_in': nrm(ks[8], (DEPTH, D_MODEL, IN_COLS), D_MODEL ** -0.5),
        'b_gate': nrm(ks[9], (DEPTH, 2 * D_MODEL), 0.1),
        'gmlp_ln_g': 1.0 + nrm(ks[10], (DEPTH, E_A), 0.02),
        'gmlp_ln_b': nrm(ks[11], (DEPTH, E_A), 0.02),
        'w_s': nrm(ks[12], (DEPTH, G_A, CHUNK, CHUNK), CHUNK ** -0.5),
        'b_s': 1.0 + nrm(ks[13], (DEPTH, G_A, CHUNK), 0.02),
        'lambda_qk': nrm(ks[14], (DEPTH, 4, DH), 0.1),
        'subln_g': 1.0 + nrm(ks[15], (DEPTH, DV), 0.02),
        'w_pa': nrm(ks[16], (DEPTH, E_A, D_MODEL), E_A ** -0.5),
        'w_pb': nrm(ks[17], (DEPTH, E_B, D_MODEL), E_B ** -0.5),
        'w_o': nrm(ks[18], (DEPTH, D_MODEL, D_MODEL), BETA * D_MODEL ** -0.5),
        'ln_g': 1.0 + nrm(ks[19], (DEPTH, D_MODEL), 0.02),
        'ln_b': nrm(ks[20], (DEPTH, D_MODEL), 0.02),
        'w_pe': nrm(ks[21], (DEPTH, P_DIM, D_MODEL), P_DIM ** -0.5),
        'w_pg': nrm(ks[22], (DEPTH, D_MODEL, D_MODEL), D_MODEL ** -0.5),
        'b_pg': nrm(ks[23], (DEPTH, D_MODEL), 0.02),
    }


def reference(x_prompt, x_sample, p_prompt, p_sample, cache_k, cache_v, page_table,
              rel_table, w_in, b_gate, gmlp_ln_g, gmlp_ln_b, w_s, b_s, lambda_qk, subln_g,
              w_pa, w_pb, w_o, ln_g, ln_b, w_pe, w_pg, b_pg):
    y_prompt, y_sample = x_prompt, x_sample
    kp_rows, vp_rows, ks_rows, vs_rows, gs_rows = [], [], [], [], []
    for l in range(DEPTH):
        lam_init = _lambda_init(l)
        lq = lambda_qk[l].astype(jnp.float32)
        lam = jnp.exp(jnp.sum(lq[0] * lq[1])) - jnp.exp(jnp.sum(lq[2] * lq[3])) + lam_init
        sg = subln_g[l]
        layer_w = (w_in[l], b_gate[l], gmlp_ln_g[l], gmlp_ln_b[l], w_s[l], b_s[l],
                   w_pa[l], w_pb[l], w_o[l], ln_g[l], ln_b[l], w_pe[l], w_pg[l], b_pg[l])

        def attend_prompt(q, k, v):
            return _diff_attn_prompt(q, k, v, rel_table, lam, lam_init, sg)

        def attend_sample(q, k, v):
            return _diff_attn_sample(q, k, v, cache_k, cache_v, page_table, l,
                                     rel_table, lam, lam_init, sg)

        y_prompt, kp, vp, _ = _layer(y_prompt, p_prompt[l], attend_prompt, CHUNK, *layer_w)
        y_sample, ksm, vsm, gsm = _layer(y_sample, p_sample[l], attend_sample, y_sample.shape[1], *layer_w)
        kp_rows.append(kp)
        vp_rows.append(vp)
        ks_rows.append(ksm)
        vs_rows.append(vsm)
        gs_rows.append(gsm)
    new_k_prompt = jnp.stack(kp_rows)
    new_v_prompt = jnp.stack(vp_rows)
    new_k_sample = jnp.stack(ks_rows)
    new_v_sample = jnp.stack(vs_rows)
    new_gmlp_v_sample = jnp.stack(gs_rows)
    return (y_prompt, y_sample, new_k_prompt, new_v_prompt, new_k_sample, new_v_sample, new_gmlp_v_sample)
```

```python
import functools
import math

import jax
import jax.numpy as jnp
import numpy as np
from jax import lax
from jax.experimental import pallas as pl
from jax.experimental.pallas import tpu as pltpu

F32 = jnp.float32
BF16 = jnp.bfloat16

D_MODEL = 1024
E_A = 1024
G_A = 8
C_A = E_A // G_A
H_B = 8
DH = 64
DV = 2 * DH
E_B = H_B * DV
P_DIM = 256
PAGE_SIZE = 128
NUM_BUCKETS = 32
MAX_DISTANCE = 128
LN_EPS = 1e-5
RMS_EPS = 1e-5
ATTN_SCALE = DH ** -0.5
NEG_INF = -1e30

V7X_LANES = 128
V7X_VMEM_LIMIT_BYTES = 56 * 1024 * 1024

Q_TILE = 256
ROW_TILE = 256
PAGES_PER_STEP = 4


def _lambda_init(layer):
    return 0.8 - 0.6 * math.exp(-0.3 * layer)


def _dot(a, b):
    return jnp.dot(a, b, preferred_element_type=F32)


def _gelu(x):
    return 0.5 * x * (1.0 + lax.erf(x * (1.0 / math.sqrt(2.0))))


def _silu(x):
    return x * jax.nn.sigmoid(x)


def _layer_norm(x, g, b):
    xc = x - jnp.mean(x, axis=-1, keepdims=True)
    var = jnp.mean(xc * xc, axis=-1, keepdims=True)
    return xc * lax.rsqrt(var + LN_EPS) * g + b


def _lam(lq_ref, lam_init):
    lq = lq_ref[...]
    a = jnp.sum(lq[0:1] * lq[1:2], axis=1, keepdims=True)
    b = jnp.sum(lq[2:3] * lq[3:4], axis=1, keepdims=True)
    return jnp.exp(a) - jnp.exp(b) + lam_init


def _const_spec(shape):
    zeros = (0,) * len(shape)
    return pl.BlockSpec(shape, lambda *_: zeros, pipeline_mode=pl.Buffered(1))


def _qkv_kernel(x_ref, w_ref, q_ref, k_ref, v_ref):
    xb = x_ref[...].astype(BF16)
    q_ref[...] = (_dot(xb, w_ref[:, 0:E_B]) * ATTN_SCALE).astype(BF16)
    k_ref[...] = _dot(xb, w_ref[:, E_B:2 * E_B])
    v_ref[...] = _dot(xb, w_ref[:, 2 * E_B:3 * E_B])


def _qkv_proj(x2d, w_qkv):
    n = x2d.shape[0]
    tm = min(ROW_TILE, n)
    row = lambda i: (i, 0)
    return pl.pallas_call(
        _qkv_kernel,
        grid=(n // tm,),
        in_specs=[pl.BlockSpec((tm, D_MODEL), row), _const_spec((D_MODEL, 3 * E_B))],
        out_specs=[pl.BlockSpec((tm, E_B), row)] * 3,
        out_shape=[jax.ShapeDtypeStruct((n, E_B), BF16),
                   jax.ShapeDtypeStruct((n, E_B), F32),
                   jax.ShapeDtypeStruct((n, E_B), F32)],
        compiler_params=pltpu.CompilerParams(
            dimension_semantics=("arbitrary",), vmem_limit_bytes=V7X_VMEM_LIMIT_BYTES),
        name="qkv_proj",
    )(x2d, w_qkv)


def _bucket(n):
    max_exact = NUM_BUCKETS // 2
    nf = jnp.maximum(n, 1).astype(F32)
    large = max_exact + (jnp.log(nf / max_exact) / math.log(MAX_DISTANCE / max_exact)
                         * (NUM_BUCKETS - max_exact)).astype(jnp.int32)
    return jnp.where(n < max_exact, n, jnp.minimum(large, NUM_BUCKETS - 1))


def _shifted_bias(rel_table, dist, valid):
    tab = rel_table.astype(F32)
    b = tab[_bucket(jnp.maximum(dist, 0))] - tab[NUM_BUCKETS - 1]
    b = jnp.where(valid[..., None], b, NEG_INF)
    return jnp.moveaxis(b, -1, 0)


def _attn_prompt_kernel(lq_ref, g_ref, q_ref, k_ref, v_ref, bias_ref, o_ref,
                        kb_ref, vt_ref, m_ref, l_ref, acc_ref, *, lam_init):
    tq = q_ref.shape[1]
    nblk = kb_ref.shape[0]
    qi = pl.program_id(2)

    @pl.when(qi == 0)
    def _():
        for c in range(nblk):
            rows = slice(c * tq, (c + 1) * tq)
            kb_ref[c] = k_ref[0, rows, :].astype(BF16)
            vt_ref[c] = v_ref[0, rows, :].T.astype(BF16)

    qt = q_ref[0].astype(F32).T
    row = lax.broadcasted_iota(jnp.int32, qt.shape, 0)
    qcat = jnp.concatenate([jnp.where(row < DH, qt, 0.0), jnp.where(row >= DH, qt, 0.0)],
                           axis=1).astype(BF16)

    m_ref[...] = jnp.full(m_ref.shape, NEG_INF, F32)
    l_ref[...] = jnp.zeros(l_ref.shape, F32)
    acc_ref[...] = jnp.zeros(acc_ref.shape, F32)

    def body(j, carry):
        s = _dot(kb_ref[j], qcat)
        kind = jnp.clip(j - qi + 2, 0, 2)
        b = bias_ref[0, kind]
        s = s + jnp.concatenate([b, b], axis=1)
        m_old = m_ref[...]
        m_new = jnp.maximum(m_old, jnp.max(s, axis=0, keepdims=True))
        alpha = jnp.exp(m_old - m_new)
        p = jnp.exp(s - m_new)
        l_ref[...] = alpha * l_ref[...] + jnp.sum(p, axis=0, keepdims=True)
        acc_ref[...] = alpha * acc_ref[...] + _dot(vt_ref[j], p.astype(BF16))
        m_ref[...] = m_new
        return carry

    lax.fori_loop(0, qi + 1, body, 0)

    attn = acc_ref[...] / l_ref[...]
    o = (attn[:, :tq] - _lam(lq_ref, lam_init) * attn[:, tq:]).T
    o = o * lax.rsqrt(jnp.mean(o * o, axis=-1, keepdims=True) + RMS_EPS) * g_ref[...]
    o_ref[0] = (o * (1.0 - lam_init)).astype(o_ref.dtype)


def _attn_prompt(q, k, v, rel_table, lq, subln_g, lam_init):
    bsz, t, _ = q.shape
    tq = min(Q_TILE, t)
    nblk = t // tq
    r = jnp.arange(tq)[:, None]
    c = jnp.arange(tq)[None, :]
    true = jnp.ones((tq, tq), bool)
    bias = jnp.stack([jnp.zeros((H_B, tq, tq), F32),
                      _shifted_bias(rel_table, tq + c - r, true),
                      _shifted_bias(rel_table, c - r, c >= r)], axis=1)
    kernel = functools.partial(_attn_prompt_kernel, lam_init=lam_init)
    return pl.pallas_call(
        kernel,
        grid=(bsz, H_B, nblk),
        in_specs=[
            _const_spec((4, DH)),
            _const_spec((1, DV)),
            pl.BlockSpec((1, tq, DV), lambda b, h, i: (b, i, h)),
            pl.BlockSpec((1, t, DV), lambda b, h, i: (b, 0, h)),
            pl.BlockSpec((1, t, DV), lambda b, h, i: (b, 0, h)),
            pl.BlockSpec((1, 3, tq, tq), lambda b, h, i: (h, 0, 0, 0)),
        ],
        out_specs=pl.BlockSpec((1, tq, DV), lambda b, h, i: (b, i, h)),
        out_shape=jax.ShapeDtypeStruct((bsz, t, E_B), BF16),
        scratch_shapes=[
            pltpu.VMEM((nblk, tq, DV), BF16),
            pltpu.VMEM((nblk, DV, tq), BF16),
            pltpu.VMEM((1, 2 * tq), F32),
            pltpu.VMEM((1, 2 * tq), F32),
            pltpu.VMEM((DV, 2 * tq), F32),
        ],
        compiler_params=pltpu.CompilerParams(
            dimension_semantics=("arbitrary", "arbitrary", "arbitrary"),
            vmem_limit_bytes=V7X_VMEM_LIMIT_BYTES),
        name="attn_prompt",
    )(lq, subln_g.reshape(1, DV), q, k, v, bias)


def _attn_sample_kernel(pt_ref, lq_ref, g_ref, qbd_ref, kn_ref, vn_ref, bias_ref, *rest,
                        n_pages, pps, lam_init):
    kpages = rest[:pps]
    vpages = rest[pps:2 * pps]
    o_ref = rest[2 * pps]
    kself_ref, vself_ref, m_ref, l_ref, acc_ref = rest[2 * pps + 1:]
    j = pl.program_id(1)
    tdec = kn_ref.shape[1]
    ncol = qbd_ref.shape[2]
    rows_per_head = ncol // H_B

    @pl.when(j == 0)
    def _():
        m_ref[...] = jnp.full(m_ref.shape, NEG_INF, F32)
        l_ref[...] = jnp.zeros(l_ref.shape, F32)
        acc_ref[...] = jnp.zeros(acc_ref.shape, F32)
        kself_ref[...] = jnp.zeros(kself_ref.shape, BF16)
        vself_ref[...] = jnp.zeros(vself_ref.shape, BF16)
        kself_ref[0:tdec, :] = kn_ref[0].astype(BF16)
        vself_ref[0:tdec, :] = vn_ref[0].astype(BF16)

    def update(kb, vb, bias):
        s = _dot(kb, qbd_ref[0]).T
        if bias is not None:
            s = s + bias
        m_old = m_ref[...]
        m_new = jnp.maximum(m_old, jnp.max(s, axis=1, keepdims=True))
        alpha = jnp.exp(m_old - m_new)
        p = jnp.exp(s - m_new)
        l_ref[...] = alpha * l_ref[...] + jnp.sum(p, axis=1, keepdims=True)
        pv = _dot(p.astype(BF16), vb)
        for h in range(H_B):
            rows = slice(h * rows_per_head, (h + 1) * rows_per_head)
            acc_ref[rows, :] = alpha[rows] * acc_ref[rows, :] + pv[rows, h * DV:(h + 1) * DV]
        m_ref[...] = m_new

    @pl.when(j * pps < n_pages)
    def _():
        for i in range(pps):
            is_last = j * pps + i == n_pages - 1
            kb = kpages[i][0].astype(BF16)
            vb = vpages[i][0].astype(BF16)
            update(kb, vb, jnp.where(is_last, bias_ref[0], 0.0))

    @pl.when(j * pps >= n_pages)
    def _():
        update(kself_ref[...], vself_ref[...], bias_ref[1])
        attn = acc_ref[...] / l_ref[...]
        lam = _lam(lq_ref, lam_init)
        for h in range(H_B):
            r0 = h * rows_per_head
            o = attn[r0:r0 + tdec] - lam * attn[r0 + tdec:r0 + 2 * tdec]
            o = o * lax.rsqrt(jnp.mean(o * o, axis=-1, keepdims=True) + RMS_EPS) * g_ref[...]
            o_ref[0, :, h * DV:(h + 1) * DV] = (o * (1.0 - lam_init)).astype(o_ref.dtype)


def _attn_sample(q, k, v, cache_k, cache_v, page_table, layer, rel_table, lq, subln_g, lam_init):
    bsz, tdec, _ = q.shape
    n_pages = page_table.shape[1]
    n_pool = cache_k.shape[1]
    pps = PAGES_PER_STEP
    assert n_pages % pps == 0
    ncol = H_B * 2 * tdec
    assert ncol == V7X_LANES, "score columns (head, map, token) must fill the lane axis"

    q5 = q.reshape(bsz, tdec, H_B, 2, DH)
    qt = jnp.transpose(q5, (0, 2, 3, 4, 1))
    eye = jnp.eye(H_B * 2, dtype=q.dtype).reshape(H_B, 2, 1, H_B, 2, 1)
    qbd = (qt[:, :, :, :, None, None, :] * eye[None]).reshape(bsz, E_B, ncol)

    tpos = jnp.arange(tdec)[:, None]
    key = jnp.arange(PAGE_SIZE)[None, :]
    last = _shifted_bias(rel_table, PAGE_SIZE + tpos - key, jnp.ones((tdec, PAGE_SIZE), bool))
    self_ = _shifted_bias(rel_table, tpos - key, (key <= tpos) & (key < tdec))
    bias = jnp.stack([last, self_])
    bias = jnp.broadcast_to(bias[:, :, None], (2, H_B, 2, tdec, PAGE_SIZE)).reshape(2, ncol, PAGE_SIZE)

    ck = cache_k.reshape(cache_k.shape[0] * n_pool, PAGE_SIZE, E_B)
    cv = cache_v.reshape(cache_v.shape[0] * n_pool, PAGE_SIZE, E_B)
    base = layer * n_pool

    def page_map(i):
        def index(b, j, pt):
            page = jnp.minimum(j, n_pages // pps - 1) * pps + i
            return (base + pt[b, page], 0, 0)
        return index

    per_seq = lambda b, j, pt: (b, 0, 0)
    const2 = lambda b, j, pt: (0, 0)
    const3 = lambda b, j, pt: (0, 0, 0)
    page_specs = [pl.BlockSpec((1, PAGE_SIZE, E_B), page_map(i)) for i in range(pps)]
    kernel = functools.partial(_attn_sample_kernel, n_pages=n_pages, pps=pps, lam_init=lam_init)
    grid_spec = pltpu.PrefetchScalarGridSpec(
        num_scalar_prefetch=1,
        grid=(bsz, n_pages // pps + 1),
        in_specs=[
            pl.BlockSpec((4, DH), const2),
            pl.BlockSpec((1, DV), const2),
            pl.BlockSpec((1, E_B, ncol), per_seq),
            pl.BlockSpec((1, tdec, E_B), per_seq),
            pl.BlockSpec((1, tdec, E_B), per_seq),
            pl.BlockSpec((2, ncol, PAGE_SIZE), const3),
        ] + page_specs + page_specs,
        out_specs=pl.BlockSpec((1, tdec, E_B), per_seq),
        scratch_shapes=[
            pltpu.VMEM((PAGE_SIZE, E_B), BF16),
            pltpu.VMEM((PAGE_SIZE, E_B), BF16),
            pltpu.VMEM((ncol, 1), F32),
            pltpu.VMEM((ncol, 1), F32),
            pltpu.VMEM((ncol, DV), F32),
        ],
    )
    return pl.pallas_call(
        kernel,
        grid_spec=grid_spec,
        out_shape=jax.ShapeDtypeStruct((bsz, tdec, E_B), BF16),
        compiler_params=pltpu.CompilerParams(
            dimension_semantics=("arbitrary", "arbitrary"), vmem_limit_bytes=V7X_VMEM_LIMIT_BYTES),
        name="attn_sample",
    )(page_table, lq, subln_g.reshape(1, DV), qbd, k, v, bias, *([ck] * pps), *([cv] * pps))


def _post_kernel(x_ref, attn_ref, p_ref, w_u, w_v, w_za, w_zb, w_g, b_gate, gln_g, gln_b,
                 wmix, bmix, w_pa, w_pb, w_o, ln_g, ln_b, w_pe, w_pg, b_pg, *outs, alpha):
    y_ref = outs[0]
    x = x_ref[...]
    xb = x.astype(BF16)
    tm = x.shape[0]
    ck = wmix.shape[1]

    vn = _layer_norm(_gelu(_dot(xb, w_v[...])), gln_g[...], gln_b[...])
    if len(outs) > 1:
        outs[1][...] = vn
    vnb = vn.astype(BF16)
    chunks = []
    for r in range(tm // ck):
        rows = slice(r * ck, (r + 1) * ck)
        groups = [_dot(wmix[g], vnb[rows, g * C_A:(g + 1) * C_A]) + bmix[g] for g in range(G_A)]
        chunks.append(jnp.concatenate(groups, axis=1))
    mixed = jnp.concatenate(chunks, axis=0) if len(chunks) > 1 else chunks[0]
    out_a = _gelu(_dot(xb, w_u[...])) * mixed * _silu(_dot(xb, w_za[...]))
    out_b = attn_ref[...].astype(F32) * _silu(_dot(xb, w_zb[...]))
    gate = jax.nn.sigmoid(_dot(xb, w_g[...]) + b_gate[...])
    merged = (gate[:, :D_MODEL] * _dot(out_a.astype(BF16), w_pa[...])
              + gate[:, D_MODEL:] * _dot(out_b.astype(BF16), w_pb[...]))
    x1 = _layer_norm(alpha * x + _dot(merged.astype(BF16), w_o[...]), ln_g[...], ln_b[...])
    emb = _dot(p_ref[...].astype(BF16), w_pe[...])
    y_ref[...] = x1 + jax.nn.sigmoid(_dot(x1.astype(BF16), w_pg[...]) + b_pg[...]) * emb


def _post(x2d, attn2d, p2d, weights, wmix, bmix, alpha, want_v_rows):
    n = x2d.shape[0]
    tm = min(ROW_TILE, n)
    assert tm % wmix.shape[1] == 0
    row = lambda i: (i, 0)
    consts = list(weights[:8]) + [wmix, bmix] + list(weights[8:])
    out_shape = [jax.ShapeDtypeStruct((n, D_MODEL), F32)]
    out_specs = [pl.BlockSpec((tm, D_MODEL), row)]
    if want_v_rows:
        out_shape.append(jax.ShapeDtypeStruct((n, E_A), F32))
        out_specs.append(pl.BlockSpec((tm, E_A), row))
    res = pl.pallas_call(
        functools.partial(_post_kernel, alpha=alpha),
        grid=(n // tm,),
        in_specs=[pl.BlockSpec((tm, D_MODEL), row), pl.BlockSpec((tm, E_B), row),
                  pl.BlockSpec((tm, P_DIM), row)] + [_const_spec(c.shape) for c in consts],
        out_specs=out_specs,
        out_shape=out_shape,
        compiler_params=pltpu.CompilerParams(
            dimension_semantics=("arbitrary",), vmem_limit_bytes=V7X_VMEM_LIMIT_BYTES),
        name="post",
    )(x2d, attn2d, p2d, *consts)
    return res if want_v_rows else (res[0], None)


def kernel(x_prompt, x_sample, p_prompt, p_sample, cache_k, cache_v, page_table, rel_table, w_in, b_gate, gmlp_ln_g, gmlp_ln_b, w_s, b_s, lambda_qk, subln_g, w_pa, w_pb, w_o, ln_g, ln_b, w_pe, w_pg, b_pg):
    depth = w_in.shape[0]
    alpha = (2.0 * depth) ** 0.25
    bsz, seq, _ = x_prompt.shape
    dbsz, dseq, _ = x_sample.shape
    chunk = w_s.shape[-1]
    n_seq_tile = min(ROW_TILE, dbsz * dseq) // dseq

    y_p = x_prompt.reshape(bsz * seq, D_MODEL)
    y_s = x_sample.reshape(dbsz * dseq, D_MODEL)
    kp_rows, vp_rows, ks_rows, vs_rows, gs_rows = [], [], [], [], []
    for l in range(depth):
        lam_init = _lambda_init(l)
        wl = w_in[l].astype(BF16)
        o = 3 * E_A
        w_qkv = wl[:, o:o + 3 * E_B]
        row2 = lambda a: a.reshape(1, -1).astype(F32)
        weights = (wl[:, 0:E_A], wl[:, E_A:2 * E_A], wl[:, 2 * E_A:3 * E_A],
                   wl[:, o + 3 * E_B:o + 4 * E_B], wl[:, o + 4 * E_B:],
                   row2(b_gate[l]), row2(gmlp_ln_g[l]), row2(gmlp_ln_b[l]),
                   w_pa[l].astype(BF16), w_pb[l].astype(BF16), w_o[l].astype(BF16),
                   row2(ln_g[l]), row2(ln_b[l]), w_pe[l].astype(BF16), w_pg[l].astype(BF16), row2(b_pg[l]))
        tril_p = jnp.tril(jnp.ones((chunk, chunk), bool))
        wmix_p = jnp.where(tril_p, w_s[l], 0.0).astype(BF16)
        bmix_p = jnp.broadcast_to(b_s[l][:, :, None], (G_A, chunk, C_A)).astype(F32)
        ws_d = jnp.where(jnp.tril(jnp.ones((dseq, dseq), bool)), w_s[l][:, :dseq, :dseq], 0.0)
        eye = jnp.eye(n_seq_tile, dtype=F32)
        wmix_s = (eye[None, :, None, :, None] * ws_d[:, None, :, None, :]).reshape(
            G_A, n_seq_tile * dseq, n_seq_tile * dseq).astype(BF16)
        bmix_s = jnp.broadcast_to(jnp.tile(b_s[l][:, :dseq], (1, n_seq_tile))[:, :, None],
                                  (G_A, n_seq_tile * dseq, C_A)).astype(F32)

        lq = lambda_qk[l].astype(F32)

        q, k, v = _qkv_proj(y_p, w_qkv)
        attn = _attn_prompt(q.reshape(bsz, seq, E_B), k.reshape(bsz, seq, E_B), v.reshape(bsz, seq, E_B),
                            rel_table, lq, subln_g[l], lam_init)
        y_p, _ = _post(y_p, attn.reshape(bsz * seq, E_B), p_prompt[l].reshape(bsz * seq, P_DIM),
                       weights, wmix_p, bmix_p, alpha, False)
        kp_rows.append(k.reshape(bsz, seq, H_B, DV))
        vp_rows.append(v.reshape(bsz, seq, H_B, DV))

        q, k, v = _qkv_proj(y_s, w_qkv)
        attn = _attn_sample(q.reshape(dbsz, dseq, E_B), k.reshape(dbsz, dseq, E_B), v.reshape(dbsz, dseq, E_B),
                            cache_k, cache_v, page_table, l, rel_table, lq, subln_g[l], lam_init)
        y_s, g_rows = _post(y_s, attn.reshape(dbsz * dseq, E_B), p_sample[l].reshape(dbsz * dseq, P_DIM),
                            weights, wmix_s, bmix_s, alpha, True)
        ks_rows.append(k.reshape(dbsz, dseq, H_B, DV))
        vs_rows.append(v.reshape(dbsz, dseq, H_B, DV))
        gs_rows.append(g_rows.reshape(dbsz, dseq, E_A))

    return (y_p.reshape(bsz, seq, D_MODEL), y_s.reshape(dbsz, dseq, D_MODEL),
            jnp.stack(kp_rows), jnp.stack(vp_rows), jnp.stack(ks_rows), jnp.stack(vs_rows),
            jnp.stack(gs_rows))
```

```python
import functools
import math

import jax
import jax.numpy as jnp
import numpy as np
from jax import lax
from jax.experimental import pallas as pl
from jax.experimental.pallas import tpu as pltpu

F32 = jnp.float32
BF16 = jnp.bfloat16

D_MODEL = 1024
E_A = 1024
G_A = 8
C_A = E_A // G_A
H_B = 8
DH = 64
DV = 2 * DH
E_B = H_B * DV
P_DIM = 256
PAGE_SIZE = 128
NUM_BUCKETS = 32
MAX_DISTANCE = 128
LN_EPS = 1e-5
RMS_EPS = 1e-5
ATTN_SCALE = DH ** -0.5
NEG_INF = -1e30
LOG2E = math.log2(math.e)

V7X_LANES = 128
V7X_SUBLANES = 8
V7X_BF16_ROWS = 16
V7X_VMEM_LIMIT_BYTES = 56 * 1024 * 1024

Q_TILE = 256
ROW_TILE = 256
HEADS_PER_STEP = 4
PAGES_PER_STEP = 8


def _lambda_init(layer):
    return 0.8 - 0.6 * math.exp(-0.3 * layer)


def _dot(a, b):
    return jnp.dot(a, b, preferred_element_type=F32)


def _dot_nt(a, b):
    return lax.dot_general(a, b, (((1,), (1,)), ((), ())), preferred_element_type=F32)


def _gelu(x):
    return 0.5 * x * (1.0 + lax.erf(x * (1.0 / math.sqrt(2.0))))


def _silu(x):
    return x * jax.nn.sigmoid(x)


def _layer_norm(x, g, b):
    xc = x - jnp.mean(x, axis=-1, keepdims=True)
    var = jnp.mean(xc * xc, axis=-1, keepdims=True)
    return xc * lax.rsqrt(var + LN_EPS) * g + b


def _lam(lq_ref, lam_init):
    lq = lq_ref[...]
    a = jnp.sum(lq[0:1] * lq[1:2], axis=1, keepdims=True)
    b = jnp.sum(lq[2:3] * lq[3:4], axis=1, keepdims=True)
    return jnp.exp(a) - jnp.exp(b) + lam_init


def _subln(o, g, lam_init):
    o = o * lax.rsqrt(jnp.mean(o * o, axis=-1, keepdims=True) + RMS_EPS) * g
    return o * (1.0 - lam_init)


def _const_spec(shape):
    zeros = (0,) * len(shape)
    return pl.BlockSpec(shape, lambda *_: zeros, pipeline_mode=pl.Buffered(1))


def _store_heads_major(ref, val):
    rows = val.shape[0]
    for h in range(H_B):
        ref[pl.ds(h, rows, stride=H_B), :] = val[:, h * DV:(h + 1) * DV]


def _qkv_kernel(x_ref, w_ref, q_ref, kb_ref, vb_ref, k_ref, v_ref, *, transposed):
    xb = x_ref[...].astype(BF16)
    q = _dot(xb, w_ref[:, 0:E_B]) * (ATTN_SCALE * LOG2E)
    k = _dot(xb, w_ref[:, E_B:2 * E_B])
    v = _dot(xb, w_ref[:, 2 * E_B:3 * E_B])
    _store_heads_major(k_ref, k)
    _store_heads_major(v_ref, v)
    kb_ref[...] = k.astype(kb_ref.dtype)
    if transposed:
        q_ref[0] = q.T.astype(q_ref.dtype)
        vb_ref[0] = v.T.astype(vb_ref.dtype)
    else:
        q_ref[...] = q.astype(q_ref.dtype)
        vb_ref[...] = v.astype(vb_ref.dtype)


def _qkv_proj(x2d, w_qkv, transposed):
    n = x2d.shape[0]
    tm = min(Q_TILE, n)
    row = lambda i: (i, 0)
    adt = BF16 if transposed else F32
    if transposed:
        t_shape = jax.ShapeDtypeStruct((n // tm, E_B, tm), adt)
        t_spec = pl.BlockSpec((1, E_B, tm), lambda i: (i, 0, 0))
    else:
        t_shape = jax.ShapeDtypeStruct((n, E_B), adt)
        t_spec = pl.BlockSpec((tm, E_B), row)
    return pl.pallas_call(
        functools.partial(_qkv_kernel, transposed=transposed),
        grid=(n // tm,),
        in_specs=[pl.BlockSpec((tm, D_MODEL), row), _const_spec((D_MODEL, 3 * E_B))],
        out_specs=[t_spec, pl.BlockSpec((tm, E_B), row), t_spec,
                   pl.BlockSpec((tm * H_B, DV), row), pl.BlockSpec((tm * H_B, DV), row)],
        out_shape=[t_shape, jax.ShapeDtypeStruct((n, E_B), adt), t_shape,
                   jax.ShapeDtypeStruct((n * H_B, DV), F32), jax.ShapeDtypeStruct((n * H_B, DV), F32)],
        compiler_params=pltpu.CompilerParams(
            dimension_semantics=("arbitrary",), vmem_limit_bytes=V7X_VMEM_LIMIT_BYTES),
        name="qkv_proj",
    )(x2d, w_qkv)


def _bucket_np(n):
    n = np.asarray(n)
    max_exact = NUM_BUCKETS // 2
    nf = np.maximum(n, 1).astype(np.float32)
    large = max_exact + (np.log(nf / np.float32(max_exact)) / np.float32(math.log(MAX_DISTANCE / max_exact))
                         * np.float32(NUM_BUCKETS - max_exact)).astype(np.int32)
    return np.where(n < max_exact, n, np.minimum(large, NUM_BUCKETS - 1))


def _bias_of_distance(rel_table, dist):
    dist = np.asarray(dist)
    onehot = np.eye(NUM_BUCKETS, dtype=np.float32)[_bucket_np(np.maximum(dist, 0)).reshape(-1)]
    tab = rel_table.astype(F32)
    tab = (tab - tab[NUM_BUCKETS - 1:NUM_BUCKETS]) * LOG2E
    vals = jnp.dot(jnp.asarray(onehot), tab, precision=lax.Precision.HIGHEST)
    vals = jnp.where(jnp.asarray(dist.reshape(-1, 1) >= 0), vals, NEG_INF)
    return jnp.moveaxis(vals.reshape(dist.shape + (H_B,)), -1, 0)


def _toeplitz(w, n):
    length = w.shape[1]
    a = jnp.broadcast_to(w[:, None, :], (w.shape[0], n, length))
    a = jnp.pad(a, ((0, 0), (0, 0), (0, 1))).reshape(w.shape[0], n * (length + 1))
    return a[:, :n * length].reshape(w.shape[0], n, length)


def _attn_prompt_kernel(lq_ref, g_ref, qt_ref, kb_ref, vt_ref, bias_ref, o_ref,
                        qcat_ref, m_ref, acc_ref, *, hps, lam_init):
    tq = qt_ref.shape[3]
    qi = pl.program_id(2)
    ones = jnp.ones((V7X_BF16_ROWS, tq), BF16)
    zero = jnp.zeros((DH, tq), BF16)

    for h in range(hps):
        qt = qt_ref[0, 0, h * DV:(h + 1) * DV, :]
        qcat_ref[h] = jnp.concatenate([jnp.concatenate([qt[:DH], zero], axis=0),
                                       jnp.concatenate([zero, qt[DH:]], axis=0)], axis=1)
        m_ref[h] = jnp.full(m_ref.shape[1:], NEG_INF, F32)
        acc_ref[h] = jnp.zeros(acc_ref.shape[1:], F32)

    def block(j, kind):
        ks = pl.multiple_of(j * tq, tq)
        for h in range(hps):
            s = _dot(kb_ref[0, pl.ds(ks, tq), h * DV:(h + 1) * DV], qcat_ref[h])
            if kind is not None:
                b = bias_ref[h, kind]
                s = s + jnp.concatenate([b, b], axis=1)
            m_old = m_ref[h]
            m_new = jnp.maximum(m_old, jnp.max(s, axis=0, keepdims=True))
            alpha = jnp.exp2(m_old - m_new)
            p = jnp.exp2(s - m_new).astype(BF16)
            vext = jnp.concatenate([vt_ref[0, j, h * DV:(h + 1) * DV, :], ones], axis=0)
            acc_ref[h] = alpha * acc_ref[h] + _dot(vext, p)
            m_ref[h] = m_new

    def far(j, carry):
        block(j, None)
        return carry

    lax.fori_loop(0, jnp.maximum(qi - 1, 0), far, 0)

    @pl.when(qi >= 1)
    def _():
        block(qi - 1, 0)

    block(qi, 1)

    lam = _lam(lq_ref, lam_init)
    for h in range(hps):
        acc = acc_ref[h]
        attn = acc[:DV] * (1.0 / acc[DV:DV + 1])
        o = (attn[:, :tq] - lam * attn[:, tq:]).T
        o_ref[0, :, h * DV:(h + 1) * DV] = _subln(o, g_ref[...], lam_init).astype(o_ref.dtype)


def _attn_prompt(qt, kb, vt, rel_table, lq, subln_g, lam_init):
    bsz, nblk, _, tq = qt.shape
    t = nblk * tq
    hps = HEADS_PER_STEP
    w = _bias_of_distance(rel_table, np.arange(3 * tq) - tq)
    skew = _toeplitz(w, tq)
    bias = jnp.stack([skew[:, :, 2 * tq:], skew[:, :, tq:2 * tq]], axis=1)
    kernel = functools.partial(_attn_prompt_kernel, hps=hps, lam_init=lam_init)
    return pl.pallas_call(
        kernel,
        grid=(bsz, H_B // hps, nblk),
        in_specs=[
            _const_spec((4, DH)),
            _const_spec((1, DV)),
            pl.BlockSpec((1, 1, hps * DV, tq), lambda b, g, i: (b, i, g, 0)),
            pl.BlockSpec((1, t, hps * DV), lambda b, g, i: (b, 0, g)),
            pl.BlockSpec((1, nblk, hps * DV, tq), lambda b, g, i: (b, 0, g, 0)),
            pl.BlockSpec((hps, 2, tq, tq), lambda b, g, i: (g, 0, 0, 0)),
        ],
        out_specs=pl.BlockSpec((1, tq, hps * DV), lambda b, g, i: (b, i, g)),
        out_shape=jax.ShapeDtypeStruct((bsz, t, E_B), BF16),
        scratch_shapes=[
            pltpu.VMEM((hps, DV, 2 * tq), BF16),
            pltpu.VMEM((hps, 1, 2 * tq), F32),
            pltpu.VMEM((hps, DV + V7X_BF16_ROWS, 2 * tq), F32),
        ],
        compiler_params=pltpu.CompilerParams(
            dimension_semantics=("arbitrary", "arbitrary", "arbitrary"),
            vmem_limit_bytes=V7X_VMEM_LIMIT_BYTES),
        name="attn_prompt",
    )(lq, subln_g.reshape(1, DV), qt, kb, vt, bias)


def _attn_sample_kernel(pt_ref, lq_ref, g_ref, q_ref, kn_ref, vn_ref, bias_ref, *rest,
                        n_steps, pps, lam_init):
    kpages = rest[:pps]
    vpages = rest[pps:2 * pps]
    o_ref = rest[2 * pps]
    qbd_ref, kall_ref, vall_ref, m_ref, l_ref, acc_ref = rest[2 * pps + 1:]
    j = pl.program_id(1)
    tdec = q_ref.shape[1]
    ncol = qbd_ref.shape[0]
    rows_per_head = ncol // H_B

    @pl.when(j == 0)
    def _():
        m_ref[...] = jnp.full(m_ref.shape, NEG_INF, F32)
        l_ref[...] = jnp.zeros(l_ref.shape, F32)
        acc_ref[...] = jnp.zeros(acc_ref.shape, F32)
        qrep = jnp.concatenate([q_ref[0].astype(F32)] * (ncol // tdec), axis=0)
        r = lax.broadcasted_iota(jnp.int32, qrep.shape, 0) // tdec
        c = lax.broadcasted_iota(jnp.int32, qrep.shape, 1) // DH
        qbd_ref[...] = jnp.where(r == c, qrep, 0.0).astype(BF16)

    def update(kb, vb, tail_bias):
        s = _dot_nt(qbd_ref[...], kb)
        nk = s.shape[1]
        tail = s[:, nk - PAGE_SIZE:] + tail_bias
        s = tail if nk == PAGE_SIZE else jnp.concatenate([s[:, :nk - PAGE_SIZE], tail], axis=1)
        m_old = m_ref[...]
        m_new = jnp.maximum(m_old, jnp.max(s, axis=1, keepdims=True))
        alpha = jnp.exp2(m_old - m_new)
        p = jnp.exp2(s - m_new)
        l_ref[...] = alpha * l_ref[...] + jnp.sum(p, axis=1, keepdims=True)
        pv = _dot(p.astype(BF16), vb)
        for h in range(H_B):
            rows = slice(h * rows_per_head, (h + 1) * rows_per_head)
            acc_ref[rows, :] = alpha[rows] * acc_ref[rows, :] + pv[rows, h * DV:(h + 1) * DV]
        m_ref[...] = m_new

    @pl.when(j < n_steps)
    def _():
        for i in range(pps):
            for h in range(H_B):
                dst = (slice(i * PAGE_SIZE, (i + 1) * PAGE_SIZE), slice(h * DV, (h + 1) * DV))
                kall_ref[dst] = kpages[i][0, pl.ds(h, PAGE_SIZE, stride=H_B), :].astype(BF16)
                vall_ref[dst] = vpages[i][0, pl.ds(h, PAGE_SIZE, stride=H_B), :].astype(BF16)
        update(kall_ref[...], vall_ref[...], jnp.where(j == n_steps - 1, bias_ref[0], 0.0))

    @pl.when(j == n_steps)
    def _():
        pad = jnp.zeros((PAGE_SIZE - tdec, E_B), F32)
        update(jnp.concatenate([kn_ref[0], pad], axis=0).astype(BF16),
               jnp.concatenate([vn_ref[0], pad], axis=0).astype(BF16), bias_ref[1])
        attn = acc_ref[...] * (1.0 / l_ref[...])
        lam = _lam(lq_ref, lam_init)
        for h in range(H_B):
            r0 = h * rows_per_head
            o = attn[r0:r0 + tdec] - lam * attn[r0 + tdec:r0 + 2 * tdec]
            o_ref[0, :, h * DV:(h + 1) * DV] = _subln(o, g_ref[...], lam_init).astype(o_ref.dtype)


def _attn_sample(q, kb, vb, cache_k, cache_v, page_table, layer, rel_table, lq, subln_g, lam_init):
    bsz, tdec, _ = q.shape
    n_pages = page_table.shape[1]
    n_pool = cache_k.shape[1]
    pps = PAGES_PER_STEP
    assert n_pages % pps == 0
    n_steps = n_pages // pps
    ncol = H_B * 2 * tdec
    assert ncol == V7X_LANES, "score rows (head, map, token) must fill one lane tile after P.V"

    tpos = np.arange(tdec)[:, None]
    key = np.arange(PAGE_SIZE)[None, :]
    last = _bias_of_distance(rel_table, PAGE_SIZE + tpos - key)
    self_ = _bias_of_distance(rel_table, np.where(key < tdec, tpos - key, -1))
    bias = jnp.stack([last, self_])
    bias = jnp.broadcast_to(bias[:, :, None], (2, H_B, 2, tdec, PAGE_SIZE)).reshape(2, ncol, PAGE_SIZE)

    ck = cache_k.reshape(cache_k.shape[0] * n_pool, PAGE_SIZE * H_B, DV)
    cv = cache_v.reshape(cache_v.shape[0] * n_pool, PAGE_SIZE * H_B, DV)
    base = layer * n_pool

    def page_map(i):
        def index(b, j, pt):
            page = jnp.minimum(j, n_steps - 1) * pps + i
            return (base + pt[b, page], 0, 0)
        return index

    per_seq = lambda b, j, pt: (b, 0, 0)
    const2 = lambda b, j, pt: (0, 0)
    const3 = lambda b, j, pt: (0, 0, 0)
    page_specs = [pl.BlockSpec((1, PAGE_SIZE * H_B, DV), page_map(i)) for i in range(pps)]
    kernel = functools.partial(_attn_sample_kernel, n_steps=n_steps, pps=pps, lam_init=lam_init)
    grid_spec = pltpu.PrefetchScalarGridSpec(
        num_scalar_prefetch=1,
        grid=(bsz, n_steps + 1),
        in_specs=[
            pl.BlockSpec((4, DH), const2),
            pl.BlockSpec((1, DV), const2),
            pl.BlockSpec((1, tdec, E_B), per_seq),
            pl.BlockSpec((1, tdec, E_B), per_seq),
            pl.BlockSpec((1, tdec, E_B), per_seq),
            pl.BlockSpec((2, ncol, PAGE_SIZE), const3),
        ] + page_specs + page_specs,
        out_specs=pl.BlockSpec((1, tdec, E_B), per_seq),
        scratch_shapes=[
            pltpu.VMEM((ncol, E_B), BF16),
            pltpu.VMEM((pps * PAGE_SIZE, E_B), BF16),
            pltpu.VMEM((pps * PAGE_SIZE, E_B), BF16),
            pltpu.VMEM((ncol, 1), F32),
            pltpu.VMEM((ncol, 1), F32),
            pltpu.VMEM((ncol, DV), F32),
        ],
    )
    return pl.pallas_call(
        kernel,
        grid_spec=grid_spec,
        out_shape=jax.ShapeDtypeStruct((bsz, tdec, E_B), BF16),
        compiler_params=pltpu.CompilerParams(
            dimension_semantics=("arbitrary", "arbitrary"), vmem_limit_bytes=V7X_VMEM_LIMIT_BYTES),
        name="attn_sample",
    )(page_table, lq, subln_g.reshape(1, DV), q, kb, vb, bias, *([ck] * pps), *([cv] * pps))


def _post_kernel(x_ref, attn_ref, p_ref, w_u, w_v, w_za, w_zb, w_g, b_gate, gln_g, gln_b,
                 wmix, bmix, w_pa, w_pb, w_o, ln_g, ln_b, w_pe, w_pg, b_pg, *outs, alpha):
    y_ref = outs[0]
    x = x_ref[...]
    xb = x.astype(BF16)
    tm = x.shape[0]
    ck = wmix.shape[1]

    vn = _layer_norm(_gelu(_dot(xb, w_v[...])), gln_g[...], gln_b[...])
    if len(outs) > 1:
        outs[1][...] = vn
    vnb = vn.astype(BF16)
    chunks = []
    for r in range(tm // ck):
        rows = slice(r * ck, (r + 1) * ck)
        groups = [_dot(wmix[g], vnb[rows, g * C_A:(g + 1) * C_A]) + bmix[g] for g in range(G_A)]
        chunks.append(jnp.concatenate(groups, axis=1))
    mixed = jnp.concatenate(chunks, axis=0) if len(chunks) > 1 else chunks[0]
    out_a = _gelu(_dot(xb, w_u[...])) * mixed * _silu(_dot(xb, w_za[...]))
    out_b = attn_ref[...].astype(F32) * _silu(_dot(xb, w_zb[...]))
    gate = jax.nn.sigmoid(_dot(xb, w_g[...]) + b_gate[...])
    merged = (gate[:, :D_MODEL] * _dot(out_a.astype(BF16), w_pa[...])
              + gate[:, D_MODEL:] * _dot(out_b.astype(BF16), w_pb[...]))
    x1 = _layer_norm(alpha * x + _dot(merged.astype(BF16), w_o[...]), ln_g[...], ln_b[...])
    emb = _dot(p_ref[...].astype(BF16), w_pe[...])
    y_ref[...] = x1 + jax.nn.sigmoid(_dot(x1.astype(BF16), w_pg[...]) + b_pg[...]) * emb


def _post(x2d, attn2d, p2d, weights, wmix, bmix, alpha, want_v_rows):
    n = x2d.shape[0]
    tm = min(ROW_TILE, n)
    assert tm % wmix.shape[1] == 0
    row = lambda i: (i, 0)
    consts = list(weights[:8]) + [wmix, bmix] + list(weights[8:])
    out_shape = [jax.ShapeDtypeStruct((n, D_MODEL), F32)]
    out_specs = [pl.BlockSpec((tm, D_MODEL), row)]
    if want_v_rows:
        out_shape.append(jax.ShapeDtypeStruct((n, E_A), F32))
        out_specs.append(pl.BlockSpec((tm, E_A), row))
    res = pl.pallas_call(
        functools.partial(_post_kernel, alpha=alpha),
        grid=(n // tm,),
        in_specs=[pl.BlockSpec((tm, D_MODEL), row), pl.BlockSpec((tm, E_B), row),
                  pl.BlockSpec((tm, P_DIM), row)] + [_const_spec(c.shape) for c in consts],
        out_specs=out_specs,
        out_shape=out_shape,
        compiler_params=pltpu.CompilerParams(
            dimension_semantics=("arbitrary",), vmem_limit_bytes=V7X_VMEM_LIMIT_BYTES),
        name="post",
    )(x2d, attn2d, p2d, *consts)
    return res if want_v_rows else (res[0], None)


def kernel(x_prompt, x_sample, p_prompt, p_sample, cache_k, cache_v, page_table, rel_table, w_in, b_gate, gmlp_ln_g, gmlp_ln_b, w_s, b_s, lambda_qk, subln_g, w_pa, w_pb, w_o, ln_g, ln_b, w_pe, w_pg, b_pg):
    depth = w_in.shape[0]
    alpha = (2.0 * depth) ** 0.25
    bsz, seq, _ = x_prompt.shape
    dbsz, dseq, _ = x_sample.shape
    chunk = w_s.shape[-1]
    n_seq_tile = min(ROW_TILE, dbsz * dseq) // dseq
    nblk = seq // Q_TILE

    y_p = x_prompt.reshape(bsz * seq, D_MODEL)
    y_s = x_sample.reshape(dbsz * dseq, D_MODEL)
    kp_rows, vp_rows, ks_rows, vs_rows, gs_rows = [], [], [], [], []
    for l in range(depth):
        lam_init = _lambda_init(l)
        wl = w_in[l].astype(BF16)
        o = 3 * E_A
        w_qkv = wl[:, o:o + 3 * E_B]
        row2 = lambda a: a.reshape(1, -1).astype(F32)
        weights = (wl[:, 0:E_A], wl[:, E_A:2 * E_A], wl[:, 2 * E_A:3 * E_A],
                   wl[:, o + 3 * E_B:o + 4 * E_B], wl[:, o + 4 * E_B:],
                   row2(b_gate[l]), row2(gmlp_ln_g[l]), row2(gmlp_ln_b[l]),
                   w_pa[l].astype(BF16), w_pb[l].astype(BF16), w_o[l].astype(BF16),
                   row2(ln_g[l]), row2(ln_b[l]), w_pe[l].astype(BF16), w_pg[l].astype(BF16), row2(b_pg[l]))

        tril_p = jnp.tril(jnp.ones((chunk, chunk), bool))
        wmix_p = jnp.where(tril_p, w_s[l], 0.0).astype(BF16)
        bmix_p = jnp.broadcast_to(b_s[l][:, :, None], (G_A, chunk, C_A)).astype(F32)
        ws_d = jnp.where(jnp.tril(jnp.ones((dseq, dseq), bool)), w_s[l][:, :dseq, :dseq], 0.0)
        eye = jnp.eye(n_seq_tile, dtype=F32)
        wmix_s = (eye[None, :, None, :, None] * ws_d[:, None, :, None, :]).reshape(
            G_A, n_seq_tile * dseq, n_seq_tile * dseq).astype(BF16)
        bmix_s = jnp.broadcast_to(jnp.tile(b_s[l][:, :dseq], (1, n_seq_tile))[:, :, None],
                                  (G_A, n_seq_tile * dseq, C_A)).astype(F32)

        lq = lambda_qk[l].astype(F32)

        qt, kb, vt, k, v = _qkv_proj(y_p, w_qkv, True)
        attn = _attn_prompt(qt.reshape(bsz, nblk, E_B, Q_TILE), kb.reshape(bsz, seq, E_B),
                            vt.reshape(bsz, nblk, E_B, Q_TILE), rel_table, lq, subln_g[l], lam_init)
        y_p, _ = _post(y_p, attn.reshape(bsz * seq, E_B), p_prompt[l].reshape(bsz * seq, P_DIM),
                       weights, wmix_p, bmix_p, alpha, False)
        kp_rows.append(k.reshape(bsz, seq, H_B, DV))
        vp_rows.append(v.reshape(bsz, seq, H_B, DV))

        q, kb, vb, k, v = _qkv_proj(y_s, w_qkv, False)
        attn = _attn_sample(q.reshape(dbsz, dseq, E_B), kb.reshape(dbsz, dseq, E_B), vb.reshape(dbsz, dseq, E_B),
                            cache_k, cache_v, page_table, l, rel_table, lq, subln_g[l], lam_init)
        y_s, g_rows = _post(y_s, attn.reshape(dbsz * dseq, E_B), p_sample[l].reshape(dbsz * dseq, P_DIM),
                            weights, wmix_s, bmix_s, alpha, True)
        ks_rows.append(k.reshape(dbsz, dseq, H_B, DV))
        vs_rows.append(v.reshape(dbsz, dseq, H_B, DV))
        gs_rows.append(g_rows.reshape(dbsz, dseq, E_A))

    return (y_p.reshape(bsz, seq, D_MODEL), y_s.reshape(dbsz, dseq, D_MODEL),
            jnp.stack(kp_rows), jnp.stack(vp_rows), jnp.stack(ks_rows), jnp.stack(vs_rows),
            jnp.stack(gs_rows))
```

```python
import functools
import math

import jax
import jax.numpy as jnp
import numpy as np
from jax import lax
from jax.experimental import pallas as pl
from jax.experimental.pallas import tpu as pltpu

F32 = jnp.float32
BF16 = jnp.bfloat16

D_MODEL = 1024
E_A = 1024
G_A = 8
C_A = E_A // G_A
H_B = 8
DH = 64
DV = 2 * DH
E_B = H_B * DV
P_DIM = 256
PAGE_SIZE = 128
NUM_BUCKETS = 32
MAX_DISTANCE = 128
LN_EPS = 1e-5
RMS_EPS = 1e-5
ATTN_SCALE = DH ** -0.5
NEG_INF = -1e30
LOG2E = math.log2(math.e)

V7X_LANES = 128
V7X_SUBLANES = 8
V7X_BF16_ROWS = 16
V7X_VMEM_LIMIT_BYTES = 56 * 1024 * 1024

Q_TILE = 256
ROW_TILE = 256
HEADS_PER_STEP = 8
PAGES_PER_STEP = 16


def _lambda_init(layer):
    return 0.8 - 0.6 * math.exp(-0.3 * layer)


def _dot(a, b):
    return jnp.dot(a, b, preferred_element_type=F32)


def _dot_nt(a, b):
    return lax.dot_general(a, b, (((1,), (1,)), ((), ())), preferred_element_type=F32)


def _gelu(x):
    return 0.5 * x * (1.0 + lax.erf(x * (1.0 / math.sqrt(2.0))))


def _silu(x):
    return x * jax.nn.sigmoid(x)


def _layer_norm(x, g, b):
    xc = x - jnp.mean(x, axis=-1, keepdims=True)
    var = jnp.mean(xc * xc, axis=-1, keepdims=True)
    return xc * lax.rsqrt(var + LN_EPS) * g + b


def _lam(lq_ref, lam_init):
    lq = lq_ref[...]
    a = jnp.sum(lq[0:1] * lq[1:2], axis=1, keepdims=True)
    b = jnp.sum(lq[2:3] * lq[3:4], axis=1, keepdims=True)
    return jnp.exp(a) - jnp.exp(b) + lam_init


def _subln(o, g, lam_init):
    o = o * lax.rsqrt(jnp.mean(o * o, axis=-1, keepdims=True) + RMS_EPS) * g
    return o * (1.0 - lam_init)


def _const_spec(shape):
    zeros = (0,) * len(shape)
    return pl.BlockSpec(shape, lambda *_: zeros, pipeline_mode=pl.Buffered(1))


def _store_heads_major(ref, val):
    rows = val.shape[0]
    for h in range(H_B):
        ref[pl.ds(h, rows, stride=H_B), :] = val[:, h * DV:(h + 1) * DV]


def _qkv_kernel(x_ref, w_ref, q_ref, kb_ref, vb_ref, k_ref, v_ref, *, transposed):
    xb = x_ref[...].astype(BF16)
    q = _dot(xb, w_ref[:, 0:E_B]) * (ATTN_SCALE * LOG2E)
    k = _dot(xb, w_ref[:, E_B:2 * E_B])
    v = _dot(xb, w_ref[:, 2 * E_B:3 * E_B])
    _store_heads_major(k_ref, k)
    _store_heads_major(v_ref, v)
    kb_ref[...] = k.astype(kb_ref.dtype)
    if transposed:
        q_ref[0] = q.T.astype(q_ref.dtype)
        vb_ref[0] = v.T.astype(vb_ref.dtype)
    else:
        q_ref[...] = q.astype(q_ref.dtype)
        vb_ref[...] = v.astype(vb_ref.dtype)


def _qkv_proj(x2d, w_qkv, transposed):
    n = x2d.shape[0]
    tm = min(Q_TILE, n)
    row = lambda i: (i, 0)
    adt = BF16 if transposed else F32
    if transposed:
        t_shape = jax.ShapeDtypeStruct((n // tm, E_B, tm), adt)
        t_spec = pl.BlockSpec((1, E_B, tm), lambda i: (i, 0, 0))
    else:
        t_shape = jax.ShapeDtypeStruct((n, E_B), adt)
        t_spec = pl.BlockSpec((tm, E_B), row)
    return pl.pallas_call(
        functools.partial(_qkv_kernel, transposed=transposed),
        grid=(n // tm,),
        in_specs=[pl.BlockSpec((tm, D_MODEL), row), _const_spec((D_MODEL, 3 * E_B))],
        out_specs=[t_spec, pl.BlockSpec((tm, E_B), row), t_spec,
                   pl.BlockSpec((tm * H_B, DV), row), pl.BlockSpec((tm * H_B, DV), row)],
        out_shape=[t_shape, jax.ShapeDtypeStruct((n, E_B), adt), t_shape,
                   jax.ShapeDtypeStruct((n * H_B, DV), F32), jax.ShapeDtypeStruct((n * H_B, DV), F32)],
        compiler_params=pltpu.CompilerParams(
            dimension_semantics=("arbitrary",), vmem_limit_bytes=V7X_VMEM_LIMIT_BYTES),
        name="qkv_proj",
    )(x2d, w_qkv)


def _bucket_np(n):
    n = np.asarray(n)
    max_exact = NUM_BUCKETS // 2
    nf = np.maximum(n, 1).astype(np.float32)
    large = max_exact + (np.log(nf / np.float32(max_exact)) / np.float32(math.log(MAX_DISTANCE / max_exact))
                         * np.float32(NUM_BUCKETS - max_exact)).astype(np.int32)
    return np.where(n < max_exact, n, np.minimum(large, NUM_BUCKETS - 1))


def _bias_of_distance(rel_table, dist):
    dist = np.asarray(dist)
    onehot = np.eye(NUM_BUCKETS, dtype=np.float32)[_bucket_np(np.maximum(dist, 0)).reshape(-1)]
    tab = rel_table.astype(F32)
    tab = (tab - tab[NUM_BUCKETS - 1:NUM_BUCKETS]) * LOG2E
    vals = jnp.dot(jnp.asarray(onehot), tab, precision=lax.Precision.HIGHEST)
    vals = jnp.where(jnp.asarray(dist.reshape(-1, 1) >= 0), vals, NEG_INF)
    return jnp.moveaxis(vals.reshape(dist.shape + (H_B,)), -1, 0)


def _toeplitz(w, n):
    length = w.shape[1]
    a = jnp.broadcast_to(w[:, None, :], (w.shape[0], n, length))
    a = jnp.pad(a, ((0, 0), (0, 0), (0, 1))).reshape(w.shape[0], n * (length + 1))
    return a[:, :n * length].reshape(w.shape[0], n, length)


def _attn_prompt_kernel(lq_ref, g_ref, qt_ref, kb_ref, vt_ref, bias_ref, o_ref,
                        qcat_ref, m_ref, acc_ref, sa_ref, sb_ref, sd_ref, *, hps, lam_init):
    tq = qt_ref.shape[3]
    qi = pl.program_id(2)
    ones = jnp.ones((V7X_BF16_ROWS, tq), BF16)
    zero = jnp.zeros((DH, tq), BF16)

    for h in range(hps):
        qt = qt_ref[0, 0, h * DV:(h + 1) * DV, :]
        qcat_ref[h] = jnp.concatenate([jnp.concatenate([qt[:DH], zero], axis=0),
                                       jnp.concatenate([zero, qt[DH:]], axis=0)], axis=1)
        m_ref[h] = jnp.full(m_ref.shape[1:], NEG_INF, F32)
        acc_ref[h] = jnp.zeros(acc_ref.shape[1:], F32)

    def scores(j, s_ref):
        ks = pl.multiple_of(j * tq, tq)
        for h in range(hps):
            s_ref[h] = _dot(kb_ref[0, pl.ds(ks, tq), h * DV:(h + 1) * DV], qcat_ref[h])

    def update(j, s_ref, kind):
        for h in range(hps):
            s = s_ref[h]
            if kind is not None:
                b = bias_ref[h, kind]
                s = s + jnp.concatenate([b, b], axis=1)
            m_old = m_ref[h]
            m_new = jnp.maximum(m_old, jnp.max(s, axis=0, keepdims=True))
            alpha = jnp.exp2(m_old - m_new)
            p = jnp.exp2(s - m_new).astype(BF16)
            vext = jnp.concatenate([vt_ref[0, j, h * DV:(h + 1) * DV, :], ones], axis=0)
            acc_ref[h] = alpha * acc_ref[h] + _dot(vext, p)
            m_ref[h] = m_new

    n_far = jnp.maximum(qi - 1, 0)
    n_pair = n_far // 2
    scores(0, sa_ref)

    def far_pair(i, carry):
        j = 2 * i
        scores(j + 1, sb_ref)
        update(j, sa_ref, None)
        scores(j + 2, sa_ref)
        update(j + 1, sb_ref, None)
        return carry

    lax.fori_loop(0, n_pair, far_pair, 0)

    @pl.when(n_far % 2 == 1)
    def _():
        scores(qi - 1, sb_ref)
        update(qi - 2, sa_ref, None)
        scores(qi, sd_ref)
        update(qi - 1, sb_ref, 0)

    @pl.when((n_far % 2 == 0) & (qi >= 1))
    def _():
        scores(qi, sd_ref)
        update(qi - 1, sa_ref, 0)

    @pl.when(qi == 0)
    def _():
        scores(0, sd_ref)

    update(qi, sd_ref, 1)

    lam = _lam(lq_ref, lam_init)
    for h in range(hps):
        acc = acc_ref[h]
        attn = acc[:DV] * (1.0 / acc[DV:DV + 1])
        o = (attn[:, :tq] - lam * attn[:, tq:]).T
        o_ref[0, :, h * DV:(h + 1) * DV] = _subln(o, g_ref[...], lam_init).astype(o_ref.dtype)


def _attn_prompt(qt, kb, vt, rel_table, lq, subln_g, lam_init):
    bsz, nblk, _, tq = qt.shape
    t = nblk * tq
    hps = HEADS_PER_STEP
    w = _bias_of_distance(rel_table, np.arange(3 * tq) - tq)
    skew = _toeplitz(w, tq)
    bias = jnp.stack([skew[:, :, 2 * tq:], skew[:, :, tq:2 * tq]], axis=1)
    kernel = functools.partial(_attn_prompt_kernel, hps=hps, lam_init=lam_init)
    return pl.pallas_call(
        kernel,
        grid=(bsz, H_B // hps, nblk),
        in_specs=[
            _const_spec((4, DH)),
            _const_spec((1, DV)),
            pl.BlockSpec((1, 1, hps * DV, tq), lambda b, g, i: (b, i, g, 0)),
            pl.BlockSpec((1, t, hps * DV), lambda b, g, i: (b, 0, g)),
            pl.BlockSpec((1, nblk, hps * DV, tq), lambda b, g, i: (b, 0, g, 0)),
            pl.BlockSpec((hps, 2, tq, tq), lambda b, g, i: (g, 0, 0, 0)),
        ],
        out_specs=pl.BlockSpec((1, tq, hps * DV), lambda b, g, i: (b, i, g)),
        out_shape=jax.ShapeDtypeStruct((bsz, t, E_B), BF16),
        scratch_shapes=[
            pltpu.VMEM((hps, DV, 2 * tq), BF16),
            pltpu.VMEM((hps, 1, 2 * tq), F32),
            pltpu.VMEM((hps, DV + V7X_BF16_ROWS, 2 * tq), F32),
            pltpu.VMEM((hps, tq, 2 * tq), F32),
            pltpu.VMEM((hps, tq, 2 * tq), F32),
            pltpu.VMEM((hps, tq, 2 * tq), F32),
        ],
        compiler_params=pltpu.CompilerParams(
            dimension_semantics=("arbitrary", "arbitrary", "arbitrary"),
            vmem_limit_bytes=V7X_VMEM_LIMIT_BYTES),
        name="attn_prompt",
    )(lq, subln_g.reshape(1, DV), qt, kb, vt, bias)


def _attn_sample_kernel(pt_ref, lq_ref, g_ref, q_ref, kn_ref, vn_ref, bias_ref, *rest,
                        n_steps, pps, lam_init):
    kpages = rest[:pps]
    vpages = rest[pps:2 * pps]
    o_ref = rest[2 * pps]
    qbd_ref, kall_ref, vall_ref, m_ref, l_ref, acc_ref = rest[2 * pps + 1:]
    j = pl.program_id(1)
    tdec = q_ref.shape[1]
    ncol = qbd_ref.shape[0]
    rows_per_head = ncol // H_B

    @pl.when(j == 0)
    def _():
        m_ref[...] = jnp.full(m_ref.shape, NEG_INF, F32)
        l_ref[...] = jnp.zeros(l_ref.shape, F32)
        acc_ref[...] = jnp.zeros(acc_ref.shape, F32)
        qrep = jnp.concatenate([q_ref[0].astype(F32)] * (ncol // tdec), axis=0)
        r = lax.broadcasted_iota(jnp.int32, qrep.shape, 0) // tdec
        c = lax.broadcasted_iota(jnp.int32, qrep.shape, 1) // DH
        qbd_ref[...] = jnp.where(r == c, qrep, 0.0).astype(BF16)

    def update(kb, vb, tail_bias):
        s = _dot_nt(qbd_ref[...], kb)
        nk = s.shape[1]
        tail = s[:, nk - PAGE_SIZE:] + tail_bias
        s = tail if nk == PAGE_SIZE else jnp.concatenate([s[:, :nk - PAGE_SIZE], tail], axis=1)
        m_old = m_ref[...]
        m_new = jnp.maximum(m_old, jnp.max(s, axis=1, keepdims=True))
        alpha = jnp.exp2(m_old - m_new)
        p = jnp.exp2(s - m_new)
        l_ref[...] = alpha * l_ref[...] + jnp.sum(p, axis=1, keepdims=True)
        pv = _dot(p.astype(BF16), vb)
        for h in range(H_B):
            rows = slice(h * rows_per_head, (h + 1) * rows_per_head)
            acc_ref[rows, :] = alpha[rows] * acc_ref[rows, :] + pv[rows, h * DV:(h + 1) * DV]
        m_ref[...] = m_new

    @pl.when(j < n_steps)
    def _():
        for i in range(pps):
            for h in range(H_B):
                dst = (slice(i * PAGE_SIZE, (i + 1) * PAGE_SIZE), slice(h * DV, (h + 1) * DV))
                kall_ref[dst] = kpages[i][0, pl.ds(h, PAGE_SIZE, stride=H_B), :].astype(BF16)
                vall_ref[dst] = vpages[i][0, pl.ds(h, PAGE_SIZE, stride=H_B), :].astype(BF16)
        update(kall_ref[...], vall_ref[...], jnp.where(j == n_steps - 1, bias_ref[0], 0.0))

    @pl.when(j == n_steps)
    def _():
        pad = jnp.zeros((PAGE_SIZE - tdec, E_B), F32)
        update(jnp.concatenate([kn_ref[0], pad], axis=0).astype(BF16),
               jnp.concatenate([vn_ref[0], pad], axis=0).astype(BF16), bias_ref[1])
        attn = acc_ref[...] * (1.0 / l_ref[...])
        lam = _lam(lq_ref, lam_init)
        for h in range(H_B):
            r0 = h * rows_per_head
            o = attn[r0:r0 + tdec] - lam * attn[r0 + tdec:r0 + 2 * tdec]
            o_ref[0, :, h * DV:(h + 1) * DV] = _subln(o, g_ref[...], lam_init).astype(o_ref.dtype)


def _attn_sample(q, kb, vb, cache_k, cache_v, page_table, layer, rel_table, lq, subln_g, lam_init):
    bsz, tdec, _ = q.shape
    n_pages = page_table.shape[1]
    n_pool = cache_k.shape[1]
    pps = PAGES_PER_STEP
    assert n_pages % pps == 0
    n_steps = n_pages // pps
    ncol = H_B * 2 * tdec
    assert ncol == V7X_LANES, "score rows (head, map, token) must fill one lane tile after P.V"

    tpos = np.arange(tdec)[:, None]
    key = np.arange(PAGE_SIZE)[None, :]
    last = _bias_of_distance(rel_table, PAGE_SIZE + tpos - key)
    self_ = _bias_of_distance(rel_table, np.where(key < tdec, tpos - key, -1))
    bias = jnp.stack([last, self_])
    bias = jnp.broadcast_to(bias[:, :, None], (2, H_B, 2, tdec, PAGE_SIZE)).reshape(2, ncol, PAGE_SIZE)

    ck = cache_k.reshape(cache_k.shape[0] * n_pool, PAGE_SIZE * H_B, DV)
    cv = cache_v.reshape(cache_v.shape[0] * n_pool, PAGE_SIZE * H_B, DV)
    base = layer * n_pool

    def page_map(i):
        def index(b, j, pt):
            page = jnp.minimum(j, n_steps - 1) * pps + i
            return (base + pt[b, page], 0, 0)
        return index

    per_seq = lambda b, j, pt: (b, 0, 0)
    const2 = lambda b, j, pt: (0, 0)
    const3 = lambda b, j, pt: (0, 0, 0)
    page_specs = [pl.BlockSpec((1, PAGE_SIZE * H_B, DV), page_map(i)) for i in range(pps)]
    kernel = functools.partial(_attn_sample_kernel, n_steps=n_steps, pps=pps, lam_init=lam_init)
    grid_spec = pltpu.PrefetchScalarGridSpec(
        num_scalar_prefetch=1,
        grid=(bsz, n_steps + 1),
        in_specs=[
            pl.BlockSpec((4, DH), const2),
            pl.BlockSpec((1, DV), const2),
            pl.BlockSpec((1, tdec, E_B), per_seq),
            pl.BlockSpec((1, tdec, E_B), per_seq),
            pl.BlockSpec((1, tdec, E_B), per_seq),
            pl.BlockSpec((2, ncol, PAGE_SIZE), const3),
        ] + page_specs + page_specs,
        out_specs=pl.BlockSpec((1, tdec, E_B), per_seq),
        scratch_shapes=[
            pltpu.VMEM((ncol, E_B), BF16),
            pltpu.VMEM((pps * PAGE_SIZE, E_B), BF16),
            pltpu.VMEM((pps * PAGE_SIZE, E_B), BF16),
            pltpu.VMEM((ncol, 1), F32),
            pltpu.VMEM((ncol, 1), F32),
            pltpu.VMEM((ncol, DV), F32),
        ],
    )
    return pl.pallas_call(
        kernel,
        grid_spec=grid_spec,
        out_shape=jax.ShapeDtypeStruct((bsz, tdec, E_B), BF16),
        compiler_params=pltpu.CompilerParams(
            dimension_semantics=("arbitrary", "arbitrary"), vmem_limit_bytes=V7X_VMEM_LIMIT_BYTES),
        name="attn_sample",
    )(page_table, lq, subln_g.reshape(1, DV), q, kb, vb, bias, *([ck] * pps), *([cv] * pps))


def _post_kernel(x_ref, attn_ref, p_ref, w_u, w_v, w_za, w_zb, w_g, b_gate, gln_g, gln_b,
                 wmix, bmix, w_pa, w_pb, w_o, ln_g, ln_b, w_pe, w_pg, b_pg, *outs, alpha):
    y_ref = outs[0]
    x = x_ref[...]
    xb = x.astype(BF16)
    tm = x.shape[0]
    ck = wmix.shape[1]

    vn = _layer_norm(_gelu(_dot(xb, w_v[...])), gln_g[...], gln_b[...])
    if len(outs) > 1:
        outs[1][...] = vn
    vnb = vn.astype(BF16)
    chunks = []
    for r in range(tm // ck):
        rows = slice(r * ck, (r + 1) * ck)
        groups = [_dot(wmix[g], vnb[rows, g * C_A:(g + 1) * C_A]) + bmix[g] for g in range(G_A)]
        chunks.append(jnp.concatenate(groups, axis=1))
    mixed = jnp.concatenate(chunks, axis=0) if len(chunks) > 1 else chunks[0]
    out_a = _gelu(_dot(xb, w_u[...])) * mixed * _silu(_dot(xb, w_za[...]))
    out_b = attn_ref[...].astype(F32) * _silu(_dot(xb, w_zb[...]))
    gate = jax.nn.sigmoid(_dot(xb, w_g[...]) + b_gate[...])
    merged = (gate[:, :D_MODEL] * _dot(out_a.astype(BF16), w_pa[...])
              + gate[:, D_MODEL:] * _dot(out_b.astype(BF16), w_pb[...]))
    x1 = _layer_norm(alpha * x + _dot(merged.astype(BF16), w_o[...]), ln_g[...], ln_b[...])
    emb = _dot(p_ref[...].astype(BF16), w_pe[...])
    y_ref[...] = x1 + jax.nn.sigmoid(_dot(x1.astype(BF16), w_pg[...]) + b_pg[...]) * emb


def _post(x2d, attn2d, p2d, weights, wmix, bmix, alpha, want_v_rows):
    n = x2d.shape[0]
    tm = min(ROW_TILE, n)
    assert tm % wmix.shape[1] == 0
    row = lambda i: (i, 0)
    consts = list(weights[:8]) + [wmix, bmix] + list(weights[8:])
    out_shape = [jax.ShapeDtypeStruct((n, D_MODEL), F32)]
    out_specs = [pl.BlockSpec((tm, D_MODEL), row)]
    if want_v_rows:
        out_shape.append(jax.ShapeDtypeStruct((n, E_A), F32))
        out_specs.append(pl.BlockSpec((tm, E_A), row))
    res = pl.pallas_call(
        functools.partial(_post_kernel, alpha=alpha),
        grid=(n // tm,),
        in_specs=[pl.BlockSpec((tm, D_MODEL), row), pl.BlockSpec((tm, E_B), row),
                  pl.BlockSpec((tm, P_DIM), row)] + [_const_spec(c.shape) for c in consts],
        out_specs=out_specs,
        out_shape=out_shape,
        compiler_params=pltpu.CompilerParams(
            dimension_semantics=("arbitrary",), vmem_limit_bytes=V7X_VMEM_LIMIT_BYTES),
        name="post",
    )(x2d, attn2d, p2d, *consts)
    return res if want_v_rows else (res[0], None)


def kernel(x_prompt, x_sample, p_prompt, p_sample, cache_k, cache_v, page_table, rel_table, w_in, b_gate, gmlp_ln_g, gmlp_ln_b, w_s, b_s, lambda_qk, subln_g, w_pa, w_pb, w_o, ln_g, ln_b, w_pe, w_pg, b_pg):
    depth = w_in.shape[0]
    alpha = (2.0 * depth) ** 0.25
    bsz, seq, _ = x_prompt.shape
    dbsz, dseq, _ = x_sample.shape
    chunk = w_s.shape[-1]
    n_seq_tile = min(ROW_TILE, dbsz * dseq) // dseq
    nblk = seq // Q_TILE

    y_p = x_prompt.reshape(bsz * seq, D_MODEL)
    y_s = x_sample.reshape(dbsz * dseq, D_MODEL)
    kp_rows, vp_rows, ks_rows, vs_rows, gs_rows = [], [], [], [], []
    for l in range(depth):
        lam_init = _lambda_init(l)
        wl = w_in[l].astype(BF16)
        o = 3 * E_A
        w_qkv = wl[:, o:o + 3 * E_B]
        row2 = lambda a: a.reshape(1, -1).astype(F32)
        weights = (wl[:, 0:E_A], wl[:, E_A:2 * E_A], wl[:, 2 * E_A:3 * E_A],
                   wl[:, o + 3 * E_B:o + 4 * E_B], wl[:, o + 4 * E_B:],
                   row2(b_gate[l]), row2(gmlp_ln_g[l]), row2(gmlp_ln_b[l]),
                   w_pa[l].astype(BF16), w_pb[l].astype(BF16), w_o[l].astype(BF16),
                   row2(ln_g[l]), row2(ln_b[l]), w_pe[l].astype(BF16), w_pg[l].astype(BF16), row2(b_pg[l]))

        tril_p = jnp.tril(jnp.ones((chunk, chunk), bool))
        wmix_p = jnp.where(tril_p, w_s[l], 0.0).astype(BF16)
        bmix_p = jnp.broadcast_to(b_s[l][:, :, None], (G_A, chunk, C_A)).astype(F32)
        ws_d = jnp.where(jnp.tril(jnp.ones((dseq, dseq), bool)), w_s[l][:, :dseq, :dseq], 0.0)
        eye = jnp.eye(n_seq_tile, dtype=F32)
        wmix_s = (eye[None, :, None, :, None] * ws_d[:, None, :, None, :]).reshape(
            G_A, n_seq_tile * dseq, n_seq_tile * dseq).astype(BF16)
        bmix_s = jnp.broadcast_to(jnp.tile(b_s[l][:, :dseq], (1, n_seq_tile))[:, :, None],
                                  (G_A, n_seq_tile * dseq, C_A)).astype(F32)

        lq = lambda_qk[l].astype(F32)

        qt, kb, vt, k, v = _qkv_proj(y_p, w_qkv, True)
        attn = _attn_prompt(qt.reshape(bsz, nblk, E_B, Q_TILE), kb.reshape(bsz, seq, E_B),
                            vt.reshape(bsz, nblk, E_B, Q_TILE), rel_table, lq, subln_g[l], lam_init)
        y_p, _ = _post(y_p, attn.reshape(bsz * seq, E_B), p_prompt[l].reshape(bsz * seq, P_DIM),
                       weights, wmix_p, bmix_p, alpha, False)
        kp_rows.append(k.reshape(bsz, seq, H_B, DV))
        vp_rows.append(v.reshape(bsz, seq, H_B, DV))

        q, kb, vb, k, v = _qkv_proj(y_s, w_qkv, False)
        attn = _attn_sample(q.reshape(dbsz, dseq, E_B), kb.reshape(dbsz, dseq, E_B), vb.reshape(dbsz, dseq, E_B),
                            cache_k, cache_v, page_table, l, rel_table, lq, subln_g[l], lam_init)
        y_s, g_rows = _post(y_s, attn.reshape(dbsz * dseq, E_B), p_sample[l].reshape(dbsz * dseq, P_DIM),
                            weights, wmix_s, bmix_s, alpha, True)
        ks_rows.append(k.reshape(dbsz, dseq, H_B, DV))
        vs_rows.append(v.reshape(dbsz, dseq, H_B, DV))
        gs_rows.append(g_rows.reshape(dbsz, dseq, E_A))

    return (y_p.reshape(bsz, seq, D_MODEL), y_s.reshape(dbsz, dseq, D_MODEL),
            jnp.stack(kp_rows), jnp.stack(vp_rows), jnp.stack(ks_rows), jnp.stack(vs_rows),
            jnp.stack(gs_rows))
```

```python
import functools
import math

import jax
import jax.numpy as jnp
import numpy as np
from jax import lax
from jax.experimental import pallas as pl
from jax.experimental.pallas import tpu as pltpu

F32 = jnp.float32
BF16 = jnp.bfloat16

D_MODEL = 1024
E_A = 1024
G_A = 8
C_A = E_A // G_A
H_B = 8
DH = 64
DV = 2 * DH
E_B = H_B * DV
P_DIM = 256
PAGE_SIZE = 128
NUM_BUCKETS = 32
MAX_DISTANCE = 128
LN_EPS = 1e-5
RMS_EPS = 1e-5
ATTN_SCALE = DH ** -0.5
NEG_INF = -1e30
LOG2E = math.log2(math.e)

V7X_LANES = 128
V7X_SUBLANES = 8
V7X_BF16_ROWS = 16
V7X_VMEM_LIMIT_BYTES = 56 * 1024 * 1024

Q_TILE = 256
ROW_TILE = 256
HEADS_PER_STEP = 8
PAGES_PER_GROUP = 8
RING_SLOTS = 4


def _lambda_init(layer):
    return 0.8 - 0.6 * math.exp(-0.3 * layer)


def _dot(a, b):
    return jnp.dot(a, b, preferred_element_type=F32)


def _dot_nt(a, b):
    return lax.dot_general(a, b, (((1,), (1,)), ((), ())), preferred_element_type=F32)


def _gelu(x):
    return 0.5 * x * (1.0 + lax.erf(x * (1.0 / math.sqrt(2.0))))


def _silu(x):
    return x * jax.nn.sigmoid(x)


def _layer_norm(x, g, b):
    xc = x - jnp.mean(x, axis=-1, keepdims=True)
    var = jnp.mean(xc * xc, axis=-1, keepdims=True)
    return xc * lax.rsqrt(var + LN_EPS) * g + b


def _lam(lq_ref, lam_init):
    lq = lq_ref[...]
    a = jnp.sum(lq[0:1] * lq[1:2], axis=1, keepdims=True)
    b = jnp.sum(lq[2:3] * lq[3:4], axis=1, keepdims=True)
    return jnp.exp(a) - jnp.exp(b) + lam_init


def _subln(o, g, lam_init):
    o = o * lax.rsqrt(jnp.mean(o * o, axis=-1, keepdims=True) + RMS_EPS) * g
    return o * (1.0 - lam_init)


def _const_spec(shape):
    zeros = (0,) * len(shape)
    return pl.BlockSpec(shape, lambda *_: zeros, pipeline_mode=pl.Buffered(1))


def _store_heads_major(ref, val):
    rows = val.shape[0]
    for h in range(H_B):
        ref[pl.ds(h, rows, stride=H_B), :] = val[:, h * DV:(h + 1) * DV]


def _qkv_kernel(x_ref, w_ref, q_ref, kb_ref, vb_ref, k_ref, v_ref, *, transposed):
    xb = x_ref[...].astype(BF16)
    q = _dot(xb, w_ref[:, 0:E_B]) * (ATTN_SCALE * LOG2E)
    k = _dot(xb, w_ref[:, E_B:2 * E_B])
    v = _dot(xb, w_ref[:, 2 * E_B:3 * E_B])
    _store_heads_major(k_ref, k)
    _store_heads_major(v_ref, v)
    kb_ref[...] = k.astype(kb_ref.dtype)
    if transposed:
        q_ref[0] = q.T.astype(q_ref.dtype)
        vb_ref[0] = v.T.astype(vb_ref.dtype)
    else:
        q_ref[...] = q.astype(q_ref.dtype)
        vb_ref[...] = v.astype(vb_ref.dtype)


def _qkv_proj(x2d, w_qkv, transposed):
    n = x2d.shape[0]
    tm = min(Q_TILE, n)
    row = lambda i: (i, 0)
    adt = BF16 if transposed else F32
    if transposed:
        t_shape = jax.ShapeDtypeStruct((n // tm, E_B, tm), adt)
        t_spec = pl.BlockSpec((1, E_B, tm), lambda i: (i, 0, 0))
    else:
        t_shape = jax.ShapeDtypeStruct((n, E_B), adt)
        t_spec = pl.BlockSpec((tm, E_B), row)
    return pl.pallas_call(
        functools.partial(_qkv_kernel, transposed=transposed),
        grid=(n // tm,),
        in_specs=[pl.BlockSpec((tm, D_MODEL), row), _const_spec((D_MODEL, 3 * E_B))],
        out_specs=[t_spec, pl.BlockSpec((tm, E_B), row), t_spec,
                   pl.BlockSpec((tm * H_B, DV), row), pl.BlockSpec((tm * H_B, DV), row)],
        out_shape=[t_shape, jax.ShapeDtypeStruct((n, E_B), adt), t_shape,
                   jax.ShapeDtypeStruct((n * H_B, DV), F32), jax.ShapeDtypeStruct((n * H_B, DV), F32)],
        compiler_params=pltpu.CompilerParams(
            dimension_semantics=("arbitrary",), vmem_limit_bytes=V7X_VMEM_LIMIT_BYTES),
        name="qkv_proj",
    )(x2d, w_qkv)


def _bucket_np(n):
    n = np.asarray(n)
    max_exact = NUM_BUCKETS // 2
    nf = np.maximum(n, 1).astype(np.float32)
    large = max_exact + (np.log(nf / np.float32(max_exact)) / np.float32(math.log(MAX_DISTANCE / max_exact))
                         * np.float32(NUM_BUCKETS - max_exact)).astype(np.int32)
    return np.where(n < max_exact, n, np.minimum(large, NUM_BUCKETS - 1))


def _bias_of_distance(rel_table, dist):
    dist = np.asarray(dist)
    onehot = np.eye(NUM_BUCKETS, dtype=np.float32)[_bucket_np(np.maximum(dist, 0)).reshape(-1)]
    tab = rel_table.astype(F32)
    tab = (tab - tab[NUM_BUCKETS - 1:NUM_BUCKETS]) * LOG2E
    vals = jnp.dot(jnp.asarray(onehot), tab, precision=lax.Precision.HIGHEST)
    vals = jnp.where(jnp.asarray(dist.reshape(-1, 1) >= 0), vals, NEG_INF)
    return jnp.moveaxis(vals.reshape(dist.shape + (H_B,)), -1, 0)


def _toeplitz(w, n):
    length = w.shape[1]
    a = jnp.broadcast_to(w[:, None, :], (w.shape[0], n, length))
    a = jnp.pad(a, ((0, 0), (0, 0), (0, 1))).reshape(w.shape[0], n * (length + 1))
    return a[:, :n * length].reshape(w.shape[0], n, length)


def _attn_prompt_kernel(lq_ref, g_ref, qt_ref, kb_ref, vt_ref, bias_ref, o_ref,
                        qcat_ref, m_ref, acc_ref, sa_ref, sb_ref, sd_ref, *, hps, lam_init):
    tq = qt_ref.shape[3]
    qi = pl.program_id(2)
    ones = jnp.ones((V7X_BF16_ROWS, tq), BF16)
    zero = jnp.zeros((DH, tq), BF16)

    for h in range(hps):
        qt = qt_ref[0, 0, h * DV:(h + 1) * DV, :]
        qcat_ref[h] = jnp.concatenate([jnp.concatenate([qt[:DH], zero], axis=0),
                                       jnp.concatenate([zero, qt[DH:]], axis=0)], axis=1)
        m_ref[h] = jnp.full(m_ref.shape[1:], NEG_INF, F32)
        acc_ref[h] = jnp.zeros(acc_ref.shape[1:], F32)

    def scores(j, s_ref):
        ks = pl.multiple_of(j * tq, tq)
        for h in range(hps):
            s_ref[h] = _dot(kb_ref[0, pl.ds(ks, tq), h * DV:(h + 1) * DV], qcat_ref[h])

    def update(j, s_ref, kind):
        for h in range(hps):
            s = s_ref[h]
            if kind is not None:
                b = bias_ref[h, kind]
                s = s + jnp.concatenate([b, b], axis=1)
            m_old = m_ref[h]
            m_new = jnp.maximum(m_old, jnp.max(s, axis=0, keepdims=True))
            alpha = jnp.exp2(m_old - m_new)
            p = jnp.exp2(s - m_new).astype(BF16)
            vext = jnp.concatenate([vt_ref[0, j, h * DV:(h + 1) * DV, :], ones], axis=0)
            acc_ref[h] = alpha * acc_ref[h] + _dot(vext, p)
            m_ref[h] = m_new

    n_far = jnp.maximum(qi - 1, 0)
    n_pair = n_far // 2
    scores(0, sa_ref)

    def far_pair(i, carry):
        j = 2 * i
        scores(j + 1, sb_ref)
        update(j, sa_ref, None)
        scores(j + 2, sa_ref)
        update(j + 1, sb_ref, None)
        return carry

    lax.fori_loop(0, n_pair, far_pair, 0)

    @pl.when(n_far % 2 == 1)
    def _():
        scores(qi - 1, sb_ref)
        update(qi - 2, sa_ref, None)
        scores(qi, sd_ref)
        update(qi - 1, sb_ref, 0)

    @pl.when((n_far % 2 == 0) & (qi >= 1))
    def _():
        scores(qi, sd_ref)
        update(qi - 1, sa_ref, 0)

    @pl.when(qi == 0)
    def _():
        scores(0, sd_ref)

    update(qi, sd_ref, 1)

    lam = _lam(lq_ref, lam_init)
    for h in range(hps):
        acc = acc_ref[h]
        attn = acc[:DV] * (1.0 / acc[DV:DV + 1])
        o = (attn[:, :tq] - lam * attn[:, tq:]).T
        o_ref[0, :, h * DV:(h + 1) * DV] = _subln(o, g_ref[...], lam_init).astype(o_ref.dtype)


def _attn_prompt(qt, kb, vt, rel_table, lq, subln_g, lam_init):
    bsz, nblk, _, tq = qt.shape
    t = nblk * tq
    hps = HEADS_PER_STEP
    w = _bias_of_distance(rel_table, np.arange(3 * tq) - tq)
    skew = _toeplitz(w, tq)
    bias = jnp.stack([skew[:, :, 2 * tq:], skew[:, :, tq:2 * tq]], axis=1)
    kernel = functools.partial(_attn_prompt_kernel, hps=hps, lam_init=lam_init)
    return pl.pallas_call(
        kernel,
        grid=(bsz, H_B // hps, nblk),
        in_specs=[
            _const_spec((4, DH)),
            _const_spec((1, DV)),
            pl.BlockSpec((1, 1, hps * DV, tq), lambda b, g, i: (b, i, g, 0)),
            pl.BlockSpec((1, t, hps * DV), lambda b, g, i: (b, 0, g)),
            pl.BlockSpec((1, nblk, hps * DV, tq), lambda b, g, i: (b, 0, g, 0)),
            pl.BlockSpec((hps, 2, tq, tq), lambda b, g, i: (g, 0, 0, 0)),
        ],
        out_specs=pl.BlockSpec((1, tq, hps * DV), lambda b, g, i: (b, i, g)),
        out_shape=jax.ShapeDtypeStruct((bsz, t, E_B), BF16),
        scratch_shapes=[
            pltpu.VMEM((hps, DV, 2 * tq), BF16),
            pltpu.VMEM((hps, 1, 2 * tq), F32),
            pltpu.VMEM((hps, DV + V7X_BF16_ROWS, 2 * tq), F32),
            pltpu.VMEM((hps, tq, 2 * tq), F32),
            pltpu.VMEM((hps, tq, 2 * tq), F32),
            pltpu.VMEM((hps, tq, 2 * tq), F32),
        ],
        compiler_params=pltpu.CompilerParams(
            dimension_semantics=("arbitrary", "arbitrary", "arbitrary"),
            vmem_limit_bytes=V7X_VMEM_LIMIT_BYTES),
        name="attn_prompt",
    )(lq, subln_g.reshape(1, DV), qt, kb, vt, bias)


def _attn_sample_kernel(pt_ref, lq_ref, g_ref, q_ref, kn_ref, vn_ref, bias_ref, ck_hbm, cv_hbm, o_ref,
                        qbd_ref, kbuf, vbuf, kall_ref, vall_ref, m_ref, l_ref, acc_ref, sem,
                        *, n_groups, ppg, nbuf, base, lam_init):
    b = pl.program_id(0)
    n_seq = pl.num_programs(0)
    tdec = q_ref.shape[1]
    ncol = qbd_ref.shape[0]
    rows_per_head = ncol // H_B

    def group_copies(seq, g, slot):
        copies = []
        for i in range(ppg):
            page = base + pt_ref[seq, g * ppg + i]
            copies.append(pltpu.make_async_copy(ck_hbm.at[page], kbuf.at[slot, i], sem.at[0, slot]))
            copies.append(pltpu.make_async_copy(cv_hbm.at[page], vbuf.at[slot, i], sem.at[1, slot]))
        return copies

    @pl.when(b == 0)
    def _():
        for g in range(nbuf - 1):
            for cp in group_copies(0, g, g):
                cp.start()

    m_ref[...] = jnp.full(m_ref.shape, NEG_INF, F32)
    l_ref[...] = jnp.zeros(l_ref.shape, F32)
    acc_ref[...] = jnp.zeros(acc_ref.shape, F32)
    qrep = jnp.concatenate([q_ref[0]] * (ncol // tdec), axis=0)
    r = lax.broadcasted_iota(jnp.int32, qrep.shape, 0) // tdec
    c = lax.broadcasted_iota(jnp.int32, qrep.shape, 1) // DH
    qbd_ref[...] = jnp.where(r == c, qrep, 0.0).astype(BF16)

    def update(kb, vb, tail_bias):
        s = _dot_nt(qbd_ref[...], kb)
        nk = s.shape[1]
        if tail_bias is not None:
            tail = s[:, nk - PAGE_SIZE:] + tail_bias
            s = tail if nk == PAGE_SIZE else jnp.concatenate([s[:, :nk - PAGE_SIZE], tail], axis=1)
        m_old = m_ref[...]
        m_new = jnp.maximum(m_old, jnp.max(s, axis=1, keepdims=True))
        alpha = jnp.exp2(m_old - m_new)
        p = jnp.exp2(s - m_new)
        l_ref[...] = alpha * l_ref[...] + jnp.sum(p, axis=1, keepdims=True)
        pv = _dot(p.astype(BF16), vb)
        for h in range(H_B):
            rows = slice(h * rows_per_head, (h + 1) * rows_per_head)
            acc_ref[rows, :] = alpha[rows] * acc_ref[rows, :] + pv[rows, h * DV:(h + 1) * DV]
        m_ref[...] = m_new

    def ring_round(it, carry):
        for slot in range(nbuf):
            g = it * nbuf + slot
            ahead = b * n_groups + g + (nbuf - 1)

            @pl.when(ahead < n_seq * n_groups)
            def _():
                for cp in group_copies(ahead // n_groups, ahead % n_groups, (slot + nbuf - 1) % nbuf):
                    cp.start()

            for cp in group_copies(b, g, slot):
                cp.wait()
            for i in range(ppg):
                for h in range(H_B):
                    dst = (slice(i * PAGE_SIZE, (i + 1) * PAGE_SIZE), slice(h * DV, (h + 1) * DV))
                    kall_ref[dst] = kbuf[slot, i, pl.ds(h, PAGE_SIZE, stride=H_B), :].astype(BF16)
                    vall_ref[dst] = vbuf[slot, i, pl.ds(h, PAGE_SIZE, stride=H_B), :].astype(BF16)
            tail_bias = jnp.where(g == n_groups - 1, bias_ref[0], 0.0) if slot == nbuf - 1 else None
            update(kall_ref[...], vall_ref[...], tail_bias)
        return carry

    lax.fori_loop(0, n_groups // nbuf, ring_round, 0)

    pad = jnp.zeros((PAGE_SIZE - tdec, E_B), F32)
    update(jnp.concatenate([kn_ref[0], pad], axis=0).astype(BF16),
           jnp.concatenate([vn_ref[0], pad], axis=0).astype(BF16), bias_ref[1])
    attn = acc_ref[...] * (1.0 / l_ref[...])
    lam = _lam(lq_ref, lam_init)
    for h in range(H_B):
        r0 = h * rows_per_head
        o = attn[r0:r0 + tdec] - lam * attn[r0 + tdec:r0 + 2 * tdec]
        o_ref[0, :, h * DV:(h + 1) * DV] = _subln(o, g_ref[...], lam_init).astype(o_ref.dtype)


def _attn_sample(q, kb, vb, cache_k, cache_v, page_table, layer, rel_table, lq, subln_g, lam_init):
    bsz, tdec, _ = q.shape
    n_pages = page_table.shape[1]
    n_pool = cache_k.shape[1]
    ppg, nbuf = PAGES_PER_GROUP, RING_SLOTS
    assert n_pages % (ppg * nbuf) == 0
    n_groups = n_pages // ppg
    ncol = H_B * 2 * tdec
    assert ncol == V7X_LANES, "score rows (head, map, token) must fill one lane tile after P.V"

    tpos = np.arange(tdec)[:, None]
    key = np.arange(PAGE_SIZE)[None, :]
    last = _bias_of_distance(rel_table, PAGE_SIZE + tpos - key)
    self_ = _bias_of_distance(rel_table, np.where(key < tdec, tpos - key, -1))
    bias = jnp.stack([last, self_])
    bias = jnp.broadcast_to(bias[:, :, None], (2, H_B, 2, tdec, PAGE_SIZE)).reshape(2, ncol, PAGE_SIZE)

    ck = cache_k.reshape(cache_k.shape[0] * n_pool, PAGE_SIZE * H_B, DV)
    cv = cache_v.reshape(cache_v.shape[0] * n_pool, PAGE_SIZE * H_B, DV)

    per_seq = lambda b, pt: (b, 0, 0)
    const2 = lambda b, pt: (0, 0)
    const3 = lambda b, pt: (0, 0, 0)
    kernel = functools.partial(_attn_sample_kernel, n_groups=n_groups, ppg=ppg, nbuf=nbuf,
                               base=layer * n_pool, lam_init=lam_init)
    grid_spec = pltpu.PrefetchScalarGridSpec(
        num_scalar_prefetch=1,
        grid=(bsz,),
        in_specs=[
            pl.BlockSpec((4, DH), const2),
            pl.BlockSpec((1, DV), const2),
            pl.BlockSpec((1, tdec, E_B), per_seq),
            pl.BlockSpec((1, tdec, E_B), per_seq),
            pl.BlockSpec((1, tdec, E_B), per_seq),
            pl.BlockSpec((2, ncol, PAGE_SIZE), const3),
            pl.BlockSpec(memory_space=pl.ANY),
            pl.BlockSpec(memory_space=pl.ANY),
        ],
        out_specs=pl.BlockSpec((1, tdec, E_B), per_seq),
        scratch_shapes=[
            pltpu.VMEM((ncol, E_B), BF16),
            pltpu.VMEM((nbuf, ppg, PAGE_SIZE * H_B, DV), F32),
            pltpu.VMEM((nbuf, ppg, PAGE_SIZE * H_B, DV), F32),
            pltpu.VMEM((ppg * PAGE_SIZE, E_B), BF16),
            pltpu.VMEM((ppg * PAGE_SIZE, E_B), BF16),
            pltpu.VMEM((ncol, 1), F32),
            pltpu.VMEM((ncol, 1), F32),
            pltpu.VMEM((ncol, DV), F32),
            pltpu.SemaphoreType.DMA((2, nbuf)),
        ],
    )
    return pl.pallas_call(
        kernel,
        grid_spec=grid_spec,
        out_shape=jax.ShapeDtypeStruct((bsz, tdec, E_B), BF16),
        compiler_params=pltpu.CompilerParams(
            dimension_semantics=("arbitrary",), vmem_limit_bytes=V7X_VMEM_LIMIT_BYTES),
        name="attn_sample",
    )(page_table, lq, subln_g.reshape(1, DV), q, kb, vb, bias, ck, cv)


def _post_kernel(x_ref, attn_ref, p_ref, w_u, w_v, w_za, w_zb, w_g, b_gate, gln_g, gln_b,
                 wmix, bmix, w_pa, w_pb, w_o, ln_g, ln_b, w_pe, w_pg, b_pg, *outs, alpha):
    y_ref = outs[0]
    x = x_ref[...]
    xb = x.astype(BF16)
    tm = x.shape[0]
    ck = wmix.shape[1]

    vn = _layer_norm(_gelu(_dot(xb, w_v[...])), gln_g[...], gln_b[...])
    if len(outs) > 1:
        outs[1][...] = vn
    vnb = vn.astype(BF16)
    chunks = []
    for r in range(tm // ck):
        rows = slice(r * ck, (r + 1) * ck)
        groups = [_dot(wmix[g], vnb[rows, g * C_A:(g + 1) * C_A]) + bmix[g] for g in range(G_A)]
        chunks.append(jnp.concatenate(groups, axis=1))
    mixed = jnp.concatenate(chunks, axis=0) if len(chunks) > 1 else chunks[0]
    out_a = _gelu(_dot(xb, w_u[...])) * mixed * _silu(_dot(xb, w_za[...]))
    out_b = attn_ref[...].astype(F32) * _silu(_dot(xb, w_zb[...]))
    gate = jax.nn.sigmoid(_dot(xb, w_g[...]) + b_gate[...])
    merged = (gate[:, :D_MODEL] * _dot(out_a.astype(BF16), w_pa[...])
              + gate[:, D_MODEL:] * _dot(out_b.astype(BF16), w_pb[...]))
    x1 = _layer_norm(alpha * x + _dot(merged.astype(BF16), w_o[...]), ln_g[...], ln_b[...])
    emb = _dot(p_ref[...].astype(BF16), w_pe[...])
    y_ref[...] = x1 + jax.nn.sigmoid(_dot(x1.astype(BF16), w_pg[...]) + b_pg[...]) * emb


def _post(x2d, attn2d, p2d, weights, wmix, bmix, alpha, want_v_rows):
    n = x2d.shape[0]
    tm = min(ROW_TILE, n)
    assert tm % wmix.shape[1] == 0
    row = lambda i: (i, 0)
    consts = list(weights[:8]) + [wmix, bmix] + list(weights[8:])
    out_shape = [jax.ShapeDtypeStruct((n, D_MODEL), F32)]
    out_specs = [pl.BlockSpec((tm, D_MODEL), row)]
    if want_v_rows:
        out_shape.append(jax.ShapeDtypeStruct((n, E_A), F32))
        out_specs.append(pl.BlockSpec((tm, E_A), row))
    res = pl.pallas_call(
        functools.partial(_post_kernel, alpha=alpha),
        grid=(n // tm,),
        in_specs=[pl.BlockSpec((tm, D_MODEL), row), pl.BlockSpec((tm, E_B), row),
                  pl.BlockSpec((tm, P_DIM), row)] + [_const_spec(c.shape) for c in consts],
        out_specs=out_specs,
        out_shape=out_shape,
        compiler_params=pltpu.CompilerParams(
            dimension_semantics=("arbitrary",), vmem_limit_bytes=V7X_VMEM_LIMIT_BYTES),
        name="post",
    )(x2d, attn2d, p2d, *consts)
    return res if want_v_rows else (res[0], None)


def kernel(x_prompt, x_sample, p_prompt, p_sample, cache_k, cache_v, page_table, rel_table, w_in, b_gate, gmlp_ln_g, gmlp_ln_b, w_s, b_s, lambda_qk, subln_g, w_pa, w_pb, w_o, ln_g, ln_b, w_pe, w_pg, b_pg):
    depth = w_in.shape[0]
    alpha = (2.0 * depth) ** 0.25
    bsz, seq, _ = x_prompt.shape
    dbsz, dseq, _ = x_sample.shape
    chunk = w_s.shape[-1]
    n_seq_tile = min(ROW_TILE, dbsz * dseq) // dseq
    nblk = seq // Q_TILE

    y_p = x_prompt.reshape(bsz * seq, D_MODEL)
    y_s = x_sample.reshape(dbsz * dseq, D_MODEL)
    kp_rows, vp_rows, ks_rows, vs_rows, gs_rows = [], [], [], [], []
    for l in range(depth):
        lam_init = _lambda_init(l)
        wl = w_in[l].astype(BF16)
        o = 3 * E_A
        w_qkv = wl[:, o:o + 3 * E_B]
        row2 = lambda a: a.reshape(1, -1).astype(F32)
        weights = (wl[:, 0:E_A], wl[:, E_A:2 * E_A], wl[:, 2 * E_A:3 * E_A],
                   wl[:, o + 3 * E_B:o + 4 * E_B], wl[:, o + 4 * E_B:],
                   row2(b_gate[l]), row2(gmlp_ln_g[l]), row2(gmlp_ln_b[l]),
                   w_pa[l].astype(BF16), w_pb[l].astype(BF16), w_o[l].astype(BF16),
                   row2(ln_g[l]), row2(ln_b[l]), w_pe[l].astype(BF16), w_pg[l].astype(BF16), row2(b_pg[l]))

        tril_p = jnp.tril(jnp.ones((chunk, chunk), bool))
        wmix_p = jnp.where(tril_p, w_s[l], 0.0).astype(BF16)
        bmix_p = jnp.broadcast_to(b_s[l][:, :, None], (G_A, chunk, C_A)).astype(F32)
        ws_d = jnp.where(jnp.tril(jnp.ones((dseq, dseq), bool)), w_s[l][:, :dseq, :dseq], 0.0)
        eye = jnp.eye(n_seq_tile, dtype=F32)
        wmix_s = (eye[None, :, None, :, None] * ws_d[:, None, :, None, :]).reshape(
            G_A, n_seq_tile * dseq, n_seq_tile * dseq).astype(BF16)
        bmix_s = jnp.broadcast_to(jnp.tile(b_s[l][:, :dseq], (1, n_seq_tile))[:, :, None],
                                  (G_A, n_seq_tile * dseq, C_A)).astype(F32)

        lq = lambda_qk[l].astype(F32)

        qt, kb, vt, k, v = _qkv_proj(y_p, w_qkv, True)
        attn = _attn_prompt(qt.reshape(bsz, nblk, E_B, Q_TILE), kb.reshape(bsz, seq, E_B),
                            vt.reshape(bsz, nblk, E_B, Q_TILE), rel_table, lq, subln_g[l], lam_init)
        y_p, _ = _post(y_p, attn.reshape(bsz * seq, E_B), p_prompt[l].reshape(bsz * seq, P_DIM),
                       weights, wmix_p, bmix_p, alpha, False)
        kp_rows.append(k.reshape(bsz, seq, H_B, DV))
        vp_rows.append(v.reshape(bsz, seq, H_B, DV))

        q, kb, vb, k, v = _qkv_proj(y_s, w_qkv, False)
        attn = _attn_sample(q.reshape(dbsz, dseq, E_B), kb.reshape(dbsz, dseq, E_B), vb.reshape(dbsz, dseq, E_B),
                            cache_k, cache_v, page_table, l, rel_table, lq, subln_g[l], lam_init)
        y_s, g_rows = _post(y_s, attn.reshape(dbsz * dseq, E_B), p_sample[l].reshape(dbsz * dseq, P_DIM),
                            weights, wmix_s, bmix_s, alpha, True)
        ks_rows.append(k.reshape(dbsz, dseq, H_B, DV))
        vs_rows.append(v.reshape(dbsz, dseq, H_B, DV))
        gs_rows.append(g_rows.reshape(dbsz, dseq, E_A))

    return (y_p.reshape(bsz, seq, D_MODEL), y_s.reshape(dbsz, dseq, D_MODEL),
            jnp.stack(kp_rows), jnp.stack(vp_rows), jnp.stack(ks_rows), jnp.stack(vs_rows),
            jnp.stack(gs_rows))
```

```python
import functools
import math
from typing import NamedTuple

import jax
import jax.numpy as jnp
import numpy as np
from jax import lax
from jax.experimental import pallas as pl
from jax.experimental.pallas import tpu as pltpu

F32 = jnp.float32
BF16 = jnp.bfloat16

D_MODEL = 1024
E_A = 1024
G_A = 8
C_A = E_A // G_A
H_B = 8
DH = 64
DV = 2 * DH
E_B = H_B * DV
P_DIM = 256
PAGE_SIZE = 128
NUM_BUCKETS = 32
MAX_DISTANCE = 128
LN_EPS = 1e-5
RMS_EPS = 1e-5
ATTN_SCALE = DH ** -0.5
NEG_INF = -1e30
LOG2E = math.log2(math.e)

V7X_LANES = 128
V7X_SUBLANES = 8
V7X_BF16_ROWS = 16
V7X_VMEM_LIMIT_BYTES = 56 * 1024 * 1024

Q_TILE = 256
ROW_TILE = 256
HEADS_PER_STEP = 8
PAGES_PER_GROUP = 4
RING_SLOTS = 4
POST_PHASES = 8
POST_CHUNKS = 1


def _lambda_init(layer):
    return 0.8 - 0.6 * math.exp(-0.3 * layer)


def _dot(a, b):
    return jnp.dot(a, b, preferred_element_type=F32)


def _dot_nt(a, b):
    return lax.dot_general(a, b, (((1,), (1,)), ((), ())), preferred_element_type=F32)


def _gelu(x):
    return 0.5 * x * (1.0 + lax.erf(x * (1.0 / math.sqrt(2.0))))


def _silu(x):
    return x * jax.nn.sigmoid(x)


def _layer_norm(x, g, b):
    xc = x - jnp.mean(x, axis=-1, keepdims=True)
    var = jnp.mean(xc * xc, axis=-1, keepdims=True)
    return xc * lax.rsqrt(var + LN_EPS) * g + b


def _lam(lq_ref, lam_init):
    lq = lq_ref[...]
    a = jnp.sum(lq[0:1] * lq[1:2], axis=1, keepdims=True)
    b = jnp.sum(lq[2:3] * lq[3:4], axis=1, keepdims=True)
    return jnp.exp(a) - jnp.exp(b) + lam_init


def _subln(o, g, lam_init):
    o = o * lax.rsqrt(jnp.mean(o * o, axis=-1, keepdims=True) + RMS_EPS) * g
    return o * (1.0 - lam_init)


def _const_spec(shape):
    zeros = (0,) * len(shape)
    return pl.BlockSpec(shape, lambda *_: zeros, pipeline_mode=pl.Buffered(1))


def _store_heads_major(ref, val):
    rows = val.shape[0]
    for h in range(H_B):
        ref[pl.ds(h, rows, stride=H_B), :] = val[:, h * DV:(h + 1) * DV]


def _qkv_kernel(x_ref, w_ref, q_ref, kb_ref, vb_ref, k_ref, v_ref, *, transposed):
    xb = x_ref[...].astype(BF16)
    q = _dot(xb, w_ref[:, 0:E_B]) * (ATTN_SCALE * LOG2E)
    k = _dot(xb, w_ref[:, E_B:2 * E_B])
    v = _dot(xb, w_ref[:, 2 * E_B:3 * E_B])
    _store_heads_major(k_ref, k)
    _store_heads_major(v_ref, v)
    kb_ref[...] = k.astype(kb_ref.dtype)
    if transposed:
        q_ref[0] = q.T.astype(q_ref.dtype)
        vb_ref[0] = v.T.astype(vb_ref.dtype)
    else:
        q_ref[...] = q.astype(q_ref.dtype)
        vb_ref[...] = v.astype(vb_ref.dtype)


def _qkv_proj(x2d, w_qkv, transposed):
    n = x2d.shape[0]
    tm = min(Q_TILE, n)
    row = lambda i: (i, 0)
    adt = BF16 if transposed else F32
    if transposed:
        t_shape = jax.ShapeDtypeStruct((n // tm, E_B, tm), adt)
        t_spec = pl.BlockSpec((1, E_B, tm), lambda i: (i, 0, 0))
    else:
        t_shape = jax.ShapeDtypeStruct((n, E_B), adt)
        t_spec = pl.BlockSpec((tm, E_B), row)
    return pl.pallas_call(
        functools.partial(_qkv_kernel, transposed=transposed),
        grid=(n // tm,),
        in_specs=[pl.BlockSpec((tm, D_MODEL), row), _const_spec((D_MODEL, 3 * E_B))],
        out_specs=[t_spec, pl.BlockSpec((tm, E_B), row), t_spec,
                   pl.BlockSpec((tm * H_B, DV), row), pl.BlockSpec((tm * H_B, DV), row)],
        out_shape=[t_shape, jax.ShapeDtypeStruct((n, E_B), adt), t_shape,
                   jax.ShapeDtypeStruct((n * H_B, DV), F32), jax.ShapeDtypeStruct((n * H_B, DV), F32)],
        compiler_params=pltpu.CompilerParams(
            dimension_semantics=("arbitrary",), vmem_limit_bytes=V7X_VMEM_LIMIT_BYTES),
        name="qkv_proj",
    )(x2d, w_qkv)


def _bucket_np(n):
    n = np.asarray(n)
    max_exact = NUM_BUCKETS // 2
    nf = np.maximum(n, 1).astype(np.float32)
    large = max_exact + (np.log(nf / np.float32(max_exact)) / np.float32(math.log(MAX_DISTANCE / max_exact))
                         * np.float32(NUM_BUCKETS - max_exact)).astype(np.int32)
    return np.where(n < max_exact, n, np.minimum(large, NUM_BUCKETS - 1))


def _bias_of_distance(rel_table, dist):
    dist = np.asarray(dist)
    onehot = np.eye(NUM_BUCKETS, dtype=np.float32)[_bucket_np(np.maximum(dist, 0)).reshape(-1)]
    tab = rel_table.astype(F32)
    tab = (tab - tab[NUM_BUCKETS - 1:NUM_BUCKETS]) * LOG2E
    vals = jnp.dot(jnp.asarray(onehot), tab, precision=lax.Precision.HIGHEST)
    vals = jnp.where(jnp.asarray(dist.reshape(-1, 1) >= 0), vals, NEG_INF)
    return jnp.moveaxis(vals.reshape(dist.shape + (H_B,)), -1, 0)


def _toeplitz(w, n):
    length = w.shape[1]
    a = jnp.broadcast_to(w[:, None, :], (w.shape[0], n, length))
    a = jnp.pad(a, ((0, 0), (0, 0), (0, 1))).reshape(w.shape[0], n * (length + 1))
    return a[:, :n * length].reshape(w.shape[0], n, length)


def _attn_prompt_kernel(lq_ref, g_ref, qt_ref, kb_ref, vt_ref, bias_ref, o_ref,
                        qcat_ref, m_ref, acc_ref, sa_ref, sb_ref, sd_ref, *, hps, lam_init):
    tq = qt_ref.shape[3]
    qi = pl.program_id(2)
    ones = jnp.ones((V7X_BF16_ROWS, tq), BF16)
    zero = jnp.zeros((DH, tq), BF16)

    for h in range(hps):
        qt = qt_ref[0, 0, h * DV:(h + 1) * DV, :]
        qcat_ref[h] = jnp.concatenate([jnp.concatenate([qt[:DH], zero], axis=0),
                                       jnp.concatenate([zero, qt[DH:]], axis=0)], axis=1)
        m_ref[h] = jnp.full(m_ref.shape[1:], NEG_INF, F32)
        acc_ref[h] = jnp.zeros(acc_ref.shape[1:], F32)

    def scores(j, s_ref):
        ks = pl.multiple_of(j * tq, tq)
        for h in range(hps):
            s_ref[h] = _dot(kb_ref[0, pl.ds(ks, tq), h * DV:(h + 1) * DV], qcat_ref[h])

    def update(j, s_ref, kind):
        for h in range(hps):
            s = s_ref[h]
            if kind is not None:
                b = bias_ref[h, kind]
                s = s + jnp.concatenate([b, b], axis=1)
            m_old = m_ref[h]
            m_new = jnp.maximum(m_old, jnp.max(s, axis=0, keepdims=True))
            alpha = jnp.exp2(m_old - m_new)
            p = jnp.exp2(s - m_new).astype(BF16)
            vext = jnp.concatenate([vt_ref[0, j, h * DV:(h + 1) * DV, :], ones], axis=0)
            acc_ref[h] = alpha * acc_ref[h] + _dot(vext, p)
            m_ref[h] = m_new

    n_far = jnp.maximum(qi - 1, 0)
    n_pair = n_far // 2
    scores(0, sa_ref)

    def far_pair(i, carry):
        j = 2 * i
        scores(j + 1, sb_ref)
        update(j, sa_ref, None)
        scores(j + 2, sa_ref)
        update(j + 1, sb_ref, None)
        return carry

    lax.fori_loop(0, n_pair, far_pair, 0)

    @pl.when(n_far % 2 == 1)
    def _():
        scores(qi - 1, sb_ref)
        update(qi - 2, sa_ref, None)
        scores(qi, sd_ref)
        update(qi - 1, sb_ref, 0)

    @pl.when((n_far % 2 == 0) & (qi >= 1))
    def _():
        scores(qi, sd_ref)
        update(qi - 1, sa_ref, 0)

    @pl.when(qi == 0)
    def _():
        scores(0, sd_ref)

    update(qi, sd_ref, 1)

    lam = _lam(lq_ref, lam_init)
    for h in range(hps):
        acc = acc_ref[h]
        attn = acc[:DV] * (1.0 / acc[DV:DV + 1])
        o = (attn[:, :tq] - lam * attn[:, tq:]).T
        o_ref[0, :, h * DV:(h + 1) * DV] = _subln(o, g_ref[...], lam_init).astype(o_ref.dtype)


def _attn_prompt(qt, kb, vt, rel_table, lq, subln_g, lam_init):
    bsz, nblk, _, tq = qt.shape
    t = nblk * tq
    hps = HEADS_PER_STEP
    w = _bias_of_distance(rel_table, np.arange(3 * tq) - tq)
    skew = _toeplitz(w, tq)
    bias = jnp.stack([skew[:, :, 2 * tq:], skew[:, :, tq:2 * tq]], axis=1)
    kernel = functools.partial(_attn_prompt_kernel, hps=hps, lam_init=lam_init)
    return pl.pallas_call(
        kernel,
        grid=(bsz, H_B // hps, nblk),
        in_specs=[
            _const_spec((4, DH)),
            _const_spec((1, DV)),
            pl.BlockSpec((1, 1, hps * DV, tq), lambda b, g, i: (b, i, g, 0)),
            pl.BlockSpec((1, t, hps * DV), lambda b, g, i: (b, 0, g)),
            pl.BlockSpec((1, nblk, hps * DV, tq), lambda b, g, i: (b, 0, g, 0)),
            pl.BlockSpec((hps, 2, tq, tq), lambda b, g, i: (g, 0, 0, 0)),
        ],
        out_specs=pl.BlockSpec((1, tq, hps * DV), lambda b, g, i: (b, i, g)),
        out_shape=jax.ShapeDtypeStruct((bsz, t, E_B), BF16),
        scratch_shapes=[
            pltpu.VMEM((hps, DV, 2 * tq), BF16),
            pltpu.VMEM((hps, 1, 2 * tq), F32),
            pltpu.VMEM((hps, DV + V7X_BF16_ROWS, 2 * tq), F32),
            pltpu.VMEM((hps, tq, 2 * tq), F32),
            pltpu.VMEM((hps, tq, 2 * tq), F32),
            pltpu.VMEM((hps, tq, 2 * tq), F32),
        ],
        compiler_params=pltpu.CompilerParams(
            dimension_semantics=("arbitrary", "arbitrary", "arbitrary"),
            vmem_limit_bytes=V7X_VMEM_LIMIT_BYTES),
        name="attn_prompt",
    )(lq, subln_g.reshape(1, DV), qt, kb, vt, bias)


class _Stream(NamedTuple):
    n_seq: int
    n_groups: int
    ppg: int
    nbuf: int
    gps: int
    base: int
    lam_init: float


class _StreamRefs(NamedTuple):
    pt: object
    lq: object
    g: object
    q: object
    kn: object
    vn: object
    bias: object
    ck: object
    cv: object
    o: object
    qbd: object
    kbuf: object
    vbuf: object
    kall: object
    vall: object
    m: object
    l: object
    acc: object
    sem: object


def _stream_copies(cfg, r, lin, slot):
    seq = lax.div(lin, cfg.n_groups)
    first_page = lax.rem(lin, cfg.n_groups) * cfg.ppg
    copies = []
    for i in range(cfg.ppg):
        page = cfg.base + r.pt[seq, first_page + i]
        copies.append(pltpu.make_async_copy(r.ck.at[page], r.kbuf.at[slot, i], r.sem.at[0, slot]))
        copies.append(pltpu.make_async_copy(r.cv.at[page], r.vbuf.at[slot, i], r.sem.at[1, slot]))
    return copies


def _stream_init(r):
    ncol, tdec = r.qbd.shape[0], r.q.shape[1]
    r.m[...] = jnp.full(r.m.shape, NEG_INF, F32)
    r.l[...] = jnp.zeros(r.l.shape, F32)
    r.acc[...] = jnp.zeros(r.acc.shape, F32)
    qrep = jnp.concatenate([r.q[0]] * (ncol // tdec), axis=0)
    row = lax.broadcasted_iota(jnp.int32, qrep.shape, 0) // tdec
    col = lax.broadcasted_iota(jnp.int32, qrep.shape, 1) // DH
    r.qbd[...] = jnp.where(row == col, qrep, 0.0).astype(BF16)


def _stream_update(r, kb, vb, tail_bias):
    rows_per_head = r.qbd.shape[0] // H_B
    s = _dot_nt(r.qbd[...], kb)
    nk = s.shape[1]
    if tail_bias is not None:
        tail = s[:, nk - PAGE_SIZE:] + tail_bias
        s = tail if nk == PAGE_SIZE else jnp.concatenate([s[:, :nk - PAGE_SIZE], tail], axis=1)
    m_old = r.m[...]
    m_new = jnp.maximum(m_old, jnp.max(s, axis=1, keepdims=True))
    alpha = jnp.exp2(m_old - m_new)
    p = jnp.exp2(s - m_new)
    r.l[...] = alpha * r.l[...] + jnp.sum(p, axis=1, keepdims=True)
    pv = _dot(p.astype(BF16), vb)
    for h in range(H_B):
        rows = slice(h * rows_per_head, (h + 1) * rows_per_head)
        r.acc[rows, :] = alpha[rows] * r.acc[rows, :] + pv[rows, h * DV:(h + 1) * DV]
    r.m[...] = m_new


def _stream_group_pieces(cfg, r, lin, slot, tail_bias):
    def fetch():
        ahead = lin + (cfg.nbuf - 1)

        @pl.when(ahead < cfg.n_seq * cfg.n_groups)
        def _():
            for cp in _stream_copies(cfg, r, ahead, (slot + cfg.nbuf - 1) % cfg.nbuf):
                cp.start()

        for cp in _stream_copies(cfg, r, lin, slot):
            cp.wait()

    def gather(i):
        for h in range(H_B):
            dst = (slice(i * PAGE_SIZE, (i + 1) * PAGE_SIZE), slice(h * DV, (h + 1) * DV))
            r.kall[dst] = r.kbuf[slot, i, pl.ds(h, PAGE_SIZE, stride=H_B), :].astype(BF16)
            r.vall[dst] = r.vbuf[slot, i, pl.ds(h, PAGE_SIZE, stride=H_B), :].astype(BF16)

    def fold():
        _stream_update(r, r.kall[...], r.vall[...], tail_bias)

    return [fetch] + [functools.partial(gather, i) for i in range(cfg.ppg)] + [fold]


def _stream_finish(cfg, r):
    tdec = r.q.shape[1]
    rows_per_head = r.qbd.shape[0] // H_B
    pad = jnp.zeros((PAGE_SIZE - tdec, E_B), F32)
    _stream_update(r, jnp.concatenate([r.kn[0], pad], axis=0).astype(BF16),
                   jnp.concatenate([r.vn[0], pad], axis=0).astype(BF16), r.bias[1])
    attn = r.acc[...] * (1.0 / r.l[...])
    lam = _lam(r.lq, cfg.lam_init)
    for h in range(H_B):
        r0 = h * rows_per_head
        o = attn[r0:r0 + tdec] - lam * attn[r0 + tdec:r0 + 2 * tdec]
        r.o[0, :, h * DV:(h + 1) * DV] = _subln(o, r.g[...], cfg.lam_init).astype(r.o.dtype)


def _stream_bias(rel_table, tdec):
    tpos = np.arange(tdec)[:, None]
    key = np.arange(PAGE_SIZE)[None, :]
    last = _bias_of_distance(rel_table, PAGE_SIZE + tpos - key)
    self_ = _bias_of_distance(rel_table, np.where(key < tdec, tpos - key, -1))
    bias = jnp.stack([last, self_])
    return jnp.broadcast_to(bias[:, :, None], (2, H_B, 2, tdec, PAGE_SIZE)).reshape(2, H_B * 2 * tdec, PAGE_SIZE)


N_POST_CONSTS = 18
N_STREAM_INPUTS = 8


def _post_kernel(pt_ref, x_ref, attn_ref, p_ref, *rest, alpha, want_v_rows, stream):
    (w_u, w_v, w_za, w_zb, w_g, b_gate, gln_g, gln_b, wmix, bmix,
     w_pa, w_pb, w_o, ln_g, ln_b, w_pe, w_pg, b_pg) = rest[:N_POST_CONSTS]
    rest = rest[N_POST_CONSTS:]
    step = pl.program_id(0)
    cfg = stream
    if cfg is not None:
        n_out = 2 + want_v_rows
        r = _StreamRefs(pt_ref, *rest[:N_STREAM_INPUTS], rest[N_STREAM_INPUTS + n_out - 1],
                        *rest[N_STREAM_INPUTS + n_out:])
        rest = rest[N_STREAM_INPUTS:]
        steps_per_seq = cfg.n_groups // cfg.gps
        seq_first = step % steps_per_seq == 0
        seq_last = step % steps_per_seq == steps_per_seq - 1

        @pl.when(step == 0)
        def _():
            for g in range(cfg.nbuf - 1):
                for cp in _stream_copies(cfg, r, g, g):
                    cp.start()

        @pl.when(seq_first)
        def _():
            _stream_init(r)

    def phase(k, col_thunks):
        pieces = []
        if cfg is not None:
            for u in range(k * cfg.gps // POST_PHASES, (k + 1) * cfg.gps // POST_PHASES):
                tail_bias = jnp.where(seq_last, r.bias[0], 0.0) if u == cfg.gps - 1 else None
                pieces += _stream_group_pieces(cfg, r, step * cfg.gps + u, u % cfg.nbuf, tail_bias)
        outs, n = [], len(col_thunks)
        for j, thunk in enumerate(col_thunks):
            for piece in pieces[j * len(pieces) // n:(j + 1) * len(pieces) // n]:
                piece()
            outs.append(thunk())
        return outs[0] if n == 1 else jnp.concatenate(outs, axis=1)

    y_ref = rest[0]
    x = x_ref[...]
    xb = x.astype(BF16)
    tm = x.shape[0]
    ck = wmix.shape[1]
    nc = POST_CHUNKS
    cw = D_MODEL // nc
    cols = [slice(c * cw, (c + 1) * cw) for c in range(nc)]

    vn = _layer_norm(phase(0, [lambda c=c: _gelu(_dot(xb, w_v[:, c])) for c in cols]),
                     gln_g[...], gln_b[...])
    if want_v_rows:
        rest[1][...] = vn
    vnb = vn.astype(BF16)

    def mix(g):
        return jnp.concatenate([_dot(wmix[g], vnb[i * ck:(i + 1) * ck, g * C_A:(g + 1) * C_A]) + bmix[g]
                                for i in range(tm // ck)], axis=0)

    mixed = phase(1, [lambda g=g: mix(g) for g in range(G_A)])
    out_a = phase(2, [lambda c=c: _gelu(_dot(xb, w_u[:, c])) * mixed[:, c] * _silu(_dot(xb, w_za[:, c]))
                      for c in cols]).astype(BF16)
    out_b = phase(3, [lambda c=c: attn_ref[:, c].astype(F32) * _silu(_dot(xb, w_zb[:, c]))
                      for c in cols]).astype(BF16)
    gcols = cols + [slice(D_MODEL + c.start, D_MODEL + c.stop) for c in cols]
    gate = phase(4, [lambda c=c: jax.nn.sigmoid(_dot(xb, w_g[:, c]) + b_gate[:, c]) for c in gcols])
    merged = phase(5, [lambda c=c, d=d: gate[:, c] * _dot(out_a, w_pa[:, c]) + gate[:, d] * _dot(out_b, w_pb[:, c])
                       for c, d in zip(cols, gcols[nc:])]).astype(BF16)
    x1 = _layer_norm(phase(6, [lambda c=c: alpha * x[:, c] + _dot(merged, w_o[:, c]) for c in cols]),
                     ln_g[...], ln_b[...])
    x1b = x1.astype(BF16)
    pb = p_ref[...].astype(BF16)
    y_ref[...] = phase(7, [lambda c=c: x1[:, c] + jax.nn.sigmoid(_dot(x1b, w_pg[:, c]) + b_pg[:, c])
                           * _dot(pb, w_pe[:, c]) for c in cols])

    if cfg is not None:
        @pl.when(seq_last)
        def _():
            _stream_finish(cfg, r)


def _post(x2d, attn2d, p2d, weights, wmix, bmix, alpha, page_table, want_v_rows=False, stream=None, stream_inputs=()):
    n = x2d.shape[0]
    tm = min(ROW_TILE, n)
    assert tm % wmix.shape[1] == 0
    row = lambda i, pt: (i, 0)
    consts = list(weights[:8]) + [wmix, bmix] + list(weights[8:])
    assert len(consts) == N_POST_CONSTS
    const_specs = [pl.BlockSpec(c.shape, functools.partial(lambda nd, i, pt: (0,) * nd, c.ndim),
                                pipeline_mode=pl.Buffered(1)) for c in consts]
    out_shape = [jax.ShapeDtypeStruct((n, D_MODEL), F32)]
    out_specs = [pl.BlockSpec((tm, D_MODEL), row)]
    if want_v_rows:
        out_shape.append(jax.ShapeDtypeStruct((n, E_A), F32))
        out_specs.append(pl.BlockSpec((tm, E_A), row))
    in_specs = [pl.BlockSpec((tm, D_MODEL), row), pl.BlockSpec((tm, E_B), row),
                pl.BlockSpec((tm, P_DIM), row)] + const_specs
    scratch = []
    if stream is not None:
        assert len(stream_inputs) == N_STREAM_INPUTS
        lq, g, q, kn, vn, bias, ck, cv = stream_inputs
        n_seq, tdec, _ = q.shape
        ncol = bias.shape[1]
        steps_per_seq = stream.n_groups // stream.gps
        assert n // tm == n_seq * steps_per_seq and stream.gps % stream.nbuf == 0
        per_seq = lambda i, pt: (i // steps_per_seq, 0, 0)
        in_specs += [
            pl.BlockSpec(lq.shape, lambda i, pt: (0, 0)),
            pl.BlockSpec(g.shape, lambda i, pt: (0, 0)),
            pl.BlockSpec((1, tdec, E_B), per_seq),
            pl.BlockSpec((1, tdec, E_B), per_seq),
            pl.BlockSpec((1, tdec, E_B), per_seq),
            pl.BlockSpec(bias.shape, lambda i, pt: (0, 0, 0)),
            pl.BlockSpec(memory_space=pl.ANY),
            pl.BlockSpec(memory_space=pl.ANY),
        ]
        out_shape.append(jax.ShapeDtypeStruct((n_seq, tdec, E_B), BF16))
        out_specs.append(pl.BlockSpec((1, tdec, E_B), per_seq))
        ring = (stream.nbuf, stream.ppg, PAGE_SIZE * H_B, DV)
        scratch = [
            pltpu.VMEM((ncol, E_B), BF16),
            pltpu.VMEM(ring, F32),
            pltpu.VMEM(ring, F32),
            pltpu.VMEM((stream.ppg * PAGE_SIZE, E_B), BF16),
            pltpu.VMEM((stream.ppg * PAGE_SIZE, E_B), BF16),
            pltpu.VMEM((ncol, 1), F32),
            pltpu.VMEM((ncol, 1), F32),
            pltpu.VMEM((ncol, DV), F32),
            pltpu.SemaphoreType.DMA((2, stream.nbuf)),
        ]
    grid_spec = pltpu.PrefetchScalarGridSpec(
        num_scalar_prefetch=1, grid=(n // tm,), in_specs=in_specs, out_specs=out_specs, scratch_shapes=scratch)
    return pl.pallas_call(
        functools.partial(_post_kernel, alpha=alpha, want_v_rows=want_v_rows, stream=stream),
        grid_spec=grid_spec,
        out_shape=out_shape,
        compiler_params=pltpu.CompilerParams(
            dimension_semantics=("arbitrary",), vmem_limit_bytes=V7X_VMEM_LIMIT_BYTES),
        name="post",
    )(page_table, x2d, attn2d, p2d, *consts, *stream_inputs)


def kernel(x_prompt, x_sample, p_prompt, p_sample, cache_k, cache_v, page_table, rel_table, w_in, b_gate, gmlp_ln_g, gmlp_ln_b, w_s, b_s, lambda_qk, subln_g, w_pa, w_pb, w_o, ln_g, ln_b, w_pe, w_pg, b_pg):
    depth = w_in.shape[0]
    alpha = (2.0 * depth) ** 0.25
    bsz, seq, _ = x_prompt.shape
    dbsz, dseq, _ = x_sample.shape
    chunk = w_s.shape[-1]
    n_seq_tile = min(ROW_TILE, dbsz * dseq) // dseq
    nblk = seq // Q_TILE
    n_pages = page_table.shape[1]
    n_pool = cache_k.shape[1]
    assert H_B * 2 * dseq == V7X_LANES, "score rows (head, map, token) must fill one lane tile after P.V"

    n_groups = n_pages // PAGES_PER_GROUP
    post_steps = bsz * seq // ROW_TILE
    assert n_pages % PAGES_PER_GROUP == 0 and (dbsz * n_groups) % post_steps == 0
    gps = dbsz * n_groups // post_steps
    assert n_groups % gps == 0, "a grid step must not straddle two sample sequences"
    ck = cache_k.reshape(depth * n_pool, PAGE_SIZE * H_B, DV)
    cv = cache_v.reshape(depth * n_pool, PAGE_SIZE * H_B, DV)
    sbias = _stream_bias(rel_table, dseq)

    y_p = x_prompt.reshape(bsz * seq, D_MODEL)
    y_s = x_sample.reshape(dbsz * dseq, D_MODEL)
    kp_rows, vp_rows, ks_rows, vs_rows, gs_rows = [], [], [], [], []
    for l in range(depth):
        lam_init = _lambda_init(l)
        wl = w_in[l].astype(BF16)
        o = 3 * E_A
        w_qkv = wl[:, o:o + 3 * E_B]
        row2 = lambda a: a.reshape(1, -1).astype(F32)
        weights = (wl[:, 0:E_A], wl[:, E_A:2 * E_A], wl[:, 2 * E_A:3 * E_A],
                   wl[:, o + 3 * E_B:o + 4 * E_B], wl[:, o + 4 * E_B:],
                   row2(b_gate[l]), row2(gmlp_ln_g[l]), row2(gmlp_ln_b[l]),
                   w_pa[l].astype(BF16), w_pb[l].astype(BF16), w_o[l].astype(BF16),
                   row2(ln_g[l]), row2(ln_b[l]), w_pe[l].astype(BF16), w_pg[l].astype(BF16), row2(b_pg[l]))

        tril_p = jnp.tril(jnp.ones((chunk, chunk), bool))
        wmix_p = jnp.where(tril_p, w_s[l], 0.0).astype(BF16)
        bmix_p = jnp.broadcast_to(b_s[l][:, :, None], (G_A, chunk, C_A)).astype(F32)
        ws_d = jnp.where(jnp.tril(jnp.ones((dseq, dseq), bool)), w_s[l][:, :dseq, :dseq], 0.0)
        eye = jnp.eye(n_seq_tile, dtype=F32)
        wmix_s = (eye[None, :, None, :, None] * ws_d[:, None, :, None, :]).reshape(
            G_A, n_seq_tile * dseq, n_seq_tile * dseq).astype(BF16)
        bmix_s = jnp.broadcast_to(jnp.tile(b_s[l][:, :dseq], (1, n_seq_tile))[:, :, None],
                                  (G_A, n_seq_tile * dseq, C_A)).astype(F32)

        lq = lambda_qk[l].astype(F32)
        g_row = subln_g[l].reshape(1, DV).astype(F32)

        qt, kb, vt, k, v = _qkv_proj(y_p, w_qkv, True)
        attn = _attn_prompt(qt.reshape(bsz, nblk, E_B, Q_TILE), kb.reshape(bsz, seq, E_B),
                            vt.reshape(bsz, nblk, E_B, Q_TILE), rel_table, lq, subln_g[l], lam_init)
        kp_rows.append(k.reshape(bsz, seq, H_B, DV))
        vp_rows.append(v.reshape(bsz, seq, H_B, DV))

        q_s, k_s, v_s, k, v = _qkv_proj(y_s, w_qkv, False)
        ks_rows.append(k.reshape(dbsz, dseq, H_B, DV))
        vs_rows.append(v.reshape(dbsz, dseq, H_B, DV))

        stream = _Stream(n_seq=dbsz, n_groups=n_groups, ppg=PAGES_PER_GROUP, nbuf=RING_SLOTS, gps=gps,
                         base=l * n_pool, lam_init=lam_init)
        stream_inputs = (lq, g_row, q_s.reshape(dbsz, dseq, E_B), k_s.reshape(dbsz, dseq, E_B),
                         v_s.reshape(dbsz, dseq, E_B), sbias, ck, cv)
        y_p, attn_s = _post(y_p, attn.reshape(bsz * seq, E_B), p_prompt[l].reshape(bsz * seq, P_DIM),
                            weights, wmix_p, bmix_p, alpha, page_table, stream=stream, stream_inputs=stream_inputs)
        y_s, g_rows = _post(y_s, attn_s.reshape(dbsz * dseq, E_B), p_sample[l].reshape(dbsz * dseq, P_DIM),
                            weights, wmix_s, bmix_s, alpha, page_table, want_v_rows=True)
        gs_rows.append(g_rows.reshape(dbsz, dseq, E_A))

    return (y_p.reshape(bsz, seq, D_MODEL), y_s.reshape(dbsz, dseq, D_MODEL),
            jnp.stack(kp_rows), jnp.stack(vp_rows), jnp.stack(ks_rows), jnp.stack(vs_rows),
            jnp.stack(gs_rows))
```

```python
import functools
import math
from typing import NamedTuple

import jax
import jax.numpy as jnp
import numpy as np
from jax import lax
from jax.experimental import pallas as pl
from jax.experimental.pallas import tpu as pltpu

F32 = jnp.float32
BF16 = jnp.bfloat16

D_MODEL = 1024
E_A = 1024
G_A = 8
C_A = E_A // G_A
H_B = 8
DH = 64
DV = 2 * DH
E_B = H_B * DV
P_DIM = 256
PAGE_SIZE = 128
NUM_BUCKETS = 32
MAX_DISTANCE = 128
LN_EPS = 1e-5
RMS_EPS = 1e-5
ATTN_SCALE = DH ** -0.5
NEG_INF = -1e30
LOG2E = math.log2(math.e)

V7X_LANES = 128
V7X_SUBLANES = 8
V7X_BF16_ROWS = 16
V7X_VMEM_LIMIT_BYTES = 60 * 1024 * 1024

Q_TILE = 256
ROW_TILE = 512
HEADS_PER_STEP = 8
PAGES_PER_GROUP = 8
RING_SLOTS = 4


def _lambda_init(layer):
    return 0.8 - 0.6 * math.exp(-0.3 * layer)


def _dot(a, b):
    return jnp.dot(a, b, preferred_element_type=F32)


def _dot_nt(a, b):
    return lax.dot_general(a, b, (((1,), (1,)), ((), ())), preferred_element_type=F32)


def _gelu(x):
    return 0.5 * x * (1.0 + lax.erf(x * (1.0 / math.sqrt(2.0))))


def _silu(x):
    return x * jax.nn.sigmoid(x)


def _layer_norm(x, g, b):
    xc = x - jnp.mean(x, axis=-1, keepdims=True)
    var = jnp.mean(xc * xc, axis=-1, keepdims=True)
    return xc * lax.rsqrt(var + LN_EPS) * g + b


def _lam(lq_ref, lam_init):
    lq = lq_ref[...]
    a = jnp.sum(lq[0:1] * lq[1:2], axis=1, keepdims=True)
    b = jnp.sum(lq[2:3] * lq[3:4], axis=1, keepdims=True)
    return jnp.exp(a) - jnp.exp(b) + lam_init


def _subln(o, g, lam_init):
    o = o * lax.rsqrt(jnp.mean(o * o, axis=-1, keepdims=True) + RMS_EPS) * g
    return o * (1.0 - lam_init)


def _const_spec(shape):
    zeros = (0,) * len(shape)
    return pl.BlockSpec(shape, lambda *_: zeros, pipeline_mode=pl.Buffered(1))


def _store_heads_major(ref, val):
    rows = val.shape[0]
    for h in range(H_B):
        ref[pl.ds(h, rows, stride=H_B), :] = val[:, h * DV:(h + 1) * DV]


def _qkv_kernel(x_ref, w_ref, q_ref, kb_ref, vb_ref, k_ref, v_ref, *, transposed):
    xb = x_ref[...].astype(BF16)
    q = _dot(xb, w_ref[:, 0:E_B]) * (ATTN_SCALE * LOG2E)
    k = _dot(xb, w_ref[:, E_B:2 * E_B])
    v = _dot(xb, w_ref[:, 2 * E_B:3 * E_B])
    _store_heads_major(k_ref, k)
    _store_heads_major(v_ref, v)
    kb_ref[...] = k.astype(kb_ref.dtype)
    if transposed:
        q_ref[0] = q.T.astype(q_ref.dtype)
        vb_ref[0] = v.T.astype(vb_ref.dtype)
    else:
        q_ref[...] = q.astype(q_ref.dtype)
        vb_ref[...] = v.astype(vb_ref.dtype)


def _qkv_proj(x2d, w_qkv, transposed):
    n = x2d.shape[0]
    tm = min(Q_TILE, n)
    row = lambda i: (i, 0)
    adt = BF16 if transposed else F32
    if transposed:
        t_shape = jax.ShapeDtypeStruct((n // tm, E_B, tm), adt)
        t_spec = pl.BlockSpec((1, E_B, tm), lambda i: (i, 0, 0))
    else:
        t_shape = jax.ShapeDtypeStruct((n, E_B), adt)
        t_spec = pl.BlockSpec((tm, E_B), row)
    return pl.pallas_call(
        functools.partial(_qkv_kernel, transposed=transposed),
        grid=(n // tm,),
        in_specs=[pl.BlockSpec((tm, D_MODEL), row), _const_spec((D_MODEL, 3 * E_B))],
        out_specs=[t_spec, pl.BlockSpec((tm, E_B), row), t_spec,
                   pl.BlockSpec((tm * H_B, DV), row), pl.BlockSpec((tm * H_B, DV), row)],
        out_shape=[t_shape, jax.ShapeDtypeStruct((n, E_B), adt), t_shape,
                   jax.ShapeDtypeStruct((n * H_B, DV), F32), jax.ShapeDtypeStruct((n * H_B, DV), F32)],
        compiler_params=pltpu.CompilerParams(
            dimension_semantics=("arbitrary",), vmem_limit_bytes=V7X_VMEM_LIMIT_BYTES),
        name="qkv_proj",
    )(x2d, w_qkv)


def _bucket_np(n):
    n = np.asarray(n)
    max_exact = NUM_BUCKETS // 2
    nf = np.maximum(n, 1).astype(np.float32)
    large = max_exact + (np.log(nf / np.float32(max_exact)) / np.float32(math.log(MAX_DISTANCE / max_exact))
                         * np.float32(NUM_BUCKETS - max_exact)).astype(np.int32)
    return np.where(n < max_exact, n, np.minimum(large, NUM_BUCKETS - 1))


def _bias_of_distance(rel_table, dist):
    dist = np.asarray(dist)
    onehot = np.eye(NUM_BUCKETS, dtype=np.float32)[_bucket_np(np.maximum(dist, 0)).reshape(-1)]
    tab = rel_table.astype(F32)
    tab = (tab - tab[NUM_BUCKETS - 1:NUM_BUCKETS]) * LOG2E
    vals = jnp.dot(jnp.asarray(onehot), tab, precision=lax.Precision.HIGHEST)
    vals = jnp.where(jnp.asarray(dist.reshape(-1, 1) >= 0), vals, NEG_INF)
    return jnp.moveaxis(vals.reshape(dist.shape + (H_B,)), -1, 0)


def _toeplitz(w, n):
    length = w.shape[1]
    a = jnp.broadcast_to(w[:, None, :], (w.shape[0], n, length))
    a = jnp.pad(a, ((0, 0), (0, 0), (0, 1))).reshape(w.shape[0], n * (length + 1))
    return a[:, :n * length].reshape(w.shape[0], n, length)


def _attn_prompt_kernel(lq_ref, g_ref, qt_ref, kb_ref, vt_ref, bias_ref, o_ref,
                        qcat_ref, m_ref, acc_ref, sa_ref, sb_ref, sd_ref, *, hps, lam_init):
    tq = qt_ref.shape[3]
    lax.fori_loop(0, qt_ref.shape[1], functools.partial(
        _attn_prompt_block, lq_ref, g_ref, qt_ref, kb_ref, vt_ref, bias_ref, o_ref,
        qcat_ref, m_ref, acc_ref, sa_ref, sb_ref, sd_ref, hps, lam_init, tq), 0)


def _attn_prompt_block(lq_ref, g_ref, qt_ref, kb_ref, vt_ref, bias_ref, o_ref,
                       qcat_ref, m_ref, acc_ref, sa_ref, sb_ref, sd_ref, hps, lam_init, tq, qi, carry):
    ones = jnp.ones((V7X_BF16_ROWS, tq), BF16)
    zero = jnp.zeros((DH, tq), BF16)

    for h in range(hps):
        qt = qt_ref[0, qi, h * DV:(h + 1) * DV, :]
        qcat_ref[h] = jnp.concatenate([jnp.concatenate([qt[:DH], zero], axis=0),
                                       jnp.concatenate([zero, qt[DH:]], axis=0)], axis=1)
        m_ref[h] = jnp.full(m_ref.shape[1:], NEG_INF, F32)
        acc_ref[h] = jnp.zeros(acc_ref.shape[1:], F32)

    def scores(j, s_ref):
        ks = pl.multiple_of(j * tq, tq)
        for h in range(hps):
            s_ref[h] = _dot(kb_ref[0, pl.ds(ks, tq), h * DV:(h + 1) * DV], qcat_ref[h])

    def update(j, s_ref, kind):
        for h in range(hps):
            s = s_ref[h]
            if kind is not None:
                b = bias_ref[h, kind]
                s = s + jnp.concatenate([b, b], axis=1)
            m_old = m_ref[h]
            m_new = jnp.maximum(m_old, jnp.max(s, axis=0, keepdims=True))
            alpha = jnp.exp2(m_old - m_new)
            p = jnp.exp2(s - m_new).astype(BF16)
            vext = jnp.concatenate([vt_ref[0, j, h * DV:(h + 1) * DV, :], ones], axis=0)
            acc_ref[h] = alpha * acc_ref[h] + _dot(vext, p)
            m_ref[h] = m_new

    n_far = jnp.maximum(qi - 1, 0)
    n_pair = n_far // 2
    scores(0, sa_ref)

    def far_pair(i, carry):
        j = 2 * i
        scores(j + 1, sb_ref)
        update(j, sa_ref, None)
        scores(j + 2, sa_ref)
        update(j + 1, sb_ref, None)
        return carry

    lax.fori_loop(0, n_pair, far_pair, 0)

    @pl.when(n_far % 2 == 1)
    def _():
        scores(qi - 1, sb_ref)
        update(qi - 2, sa_ref, None)
        scores(qi, sd_ref)
        update(qi - 1, sb_ref, 0)

    @pl.when((n_far % 2 == 0) & (qi >= 1))
    def _():
        scores(qi, sd_ref)
        update(qi - 1, sa_ref, 0)

    @pl.when(qi == 0)
    def _():
        scores(0, sd_ref)

    update(qi, sd_ref, 1)

    lam = _lam(lq_ref, lam_init)
    rows = pl.ds(pl.multiple_of(qi * tq, tq), tq)
    for h in range(hps):
        acc = acc_ref[h]
        attn = acc[:DV] * (1.0 / acc[DV:DV + 1])
        o = (attn[:, :tq] - lam * attn[:, tq:]).T
        o_ref[0, rows, h * DV:(h + 1) * DV] = _subln(o, g_ref[...], lam_init).astype(o_ref.dtype)
    return carry


def _attn_prompt(qt, kb, vt, rel_table, lq, subln_g, lam_init):
    bsz, nblk, _, tq = qt.shape
    t = nblk * tq
    hps = HEADS_PER_STEP
    w = _bias_of_distance(rel_table, np.arange(3 * tq) - tq)
    skew = _toeplitz(w, tq)
    bias = jnp.stack([skew[:, :, 2 * tq:], skew[:, :, tq:2 * tq]], axis=1)
    kernel = functools.partial(_attn_prompt_kernel, hps=hps, lam_init=lam_init)
    return pl.pallas_call(
        kernel,
        grid=(bsz, H_B // hps),
        in_specs=[
            _const_spec((4, DH)),
            _const_spec((1, DV)),
            pl.BlockSpec((1, nblk, hps * DV, tq), lambda b, g: (b, 0, g, 0)),
            pl.BlockSpec((1, t, hps * DV), lambda b, g: (b, 0, g)),
            pl.BlockSpec((1, nblk, hps * DV, tq), lambda b, g: (b, 0, g, 0)),
            pl.BlockSpec((hps, 2, tq, tq), lambda b, g: (g, 0, 0, 0)),
        ],
        out_specs=pl.BlockSpec((1, t, hps * DV), lambda b, g: (b, 0, g)),
        out_shape=jax.ShapeDtypeStruct((bsz, t, E_B), BF16),
        scratch_shapes=[
            pltpu.VMEM((hps, DV, 2 * tq), BF16),
            pltpu.VMEM((hps, 1, 2 * tq), F32),
            pltpu.VMEM((hps, DV + V7X_BF16_ROWS, 2 * tq), F32),
            pltpu.VMEM((hps, tq, 2 * tq), F32),
            pltpu.VMEM((hps, tq, 2 * tq), F32),
            pltpu.VMEM((hps, tq, 2 * tq), F32),
        ],
        compiler_params=pltpu.CompilerParams(
            dimension_semantics=("arbitrary", "arbitrary"),
            vmem_limit_bytes=V7X_VMEM_LIMIT_BYTES),
        name="attn_prompt",
    )(lq, subln_g.reshape(1, DV), qt, kb, vt, bias)


class _Stream(NamedTuple):
    n_seq: int
    n_groups: int
    ppg: int
    nbuf: int
    base: int
    lam_init: float


class _StreamRefs(NamedTuple):
    pt: object
    lq: object
    g: object
    q: object
    kn: object
    vn: object
    bias: object
    ck: object
    cv: object
    o: object
    qbd: object
    kbuf: object
    vbuf: object
    kall: object
    vall: object
    m: object
    l: object
    acc: object
    sem: object


def _stream_copies(cfg, r, lin, slot):
    seq = lax.div(lin, cfg.n_groups)
    first_page = lax.rem(lin, cfg.n_groups) * cfg.ppg
    copies = []
    for i in range(cfg.ppg):
        page = cfg.base + r.pt[seq, first_page + i]
        copies.append(pltpu.make_async_copy(r.ck.at[page], r.kbuf.at[slot, i], r.sem.at[0, slot]))
        copies.append(pltpu.make_async_copy(r.cv.at[page], r.vbuf.at[slot, i], r.sem.at[1, slot]))
    return copies


def _stream_init(r):
    ncol, tdec = r.qbd.shape[0], r.q.shape[1]
    r.m[...] = jnp.full(r.m.shape, NEG_INF, F32)
    r.l[...] = jnp.zeros(r.l.shape, F32)
    r.acc[...] = jnp.zeros(r.acc.shape, F32)
    qrep = jnp.concatenate([r.q[0]] * (ncol // tdec), axis=0)
    row = lax.broadcasted_iota(jnp.int32, qrep.shape, 0) // tdec
    col = lax.broadcasted_iota(jnp.int32, qrep.shape, 1) // DH
    r.qbd[...] = jnp.where(row == col, qrep, 0.0).astype(BF16)


def _stream_update(r, kb, vb, tail_bias):
    rows_per_head = r.qbd.shape[0] // H_B
    s = _dot_nt(r.qbd[...], kb)
    nk = s.shape[1]
    if tail_bias is not None:
        tail = s[:, nk - PAGE_SIZE:] + tail_bias
        s = tail if nk == PAGE_SIZE else jnp.concatenate([s[:, :nk - PAGE_SIZE], tail], axis=1)
    m_old = r.m[...]
    m_new = jnp.maximum(m_old, jnp.max(s, axis=1, keepdims=True))
    alpha = jnp.exp2(m_old - m_new)
    p = jnp.exp2(s - m_new)
    r.l[...] = alpha * r.l[...] + jnp.sum(p, axis=1, keepdims=True)
    pv = _dot(p.astype(BF16), vb)
    for h in range(H_B):
        rows = slice(h * rows_per_head, (h + 1) * rows_per_head)
        r.acc[rows, :] = alpha[rows] * r.acc[rows, :] + pv[rows, h * DV:(h + 1) * DV]
    r.m[...] = m_new


def _stream_group_pieces(cfg, r, lin, slot, tail_bias):
    def fetch():
        ahead = lin + (cfg.nbuf - 1)

        @pl.when(ahead < cfg.n_seq * cfg.n_groups)
        def _():
            for cp in _stream_copies(cfg, r, ahead, (slot + cfg.nbuf - 1) % cfg.nbuf):
                cp.start()

        for cp in _stream_copies(cfg, r, lin, slot):
            cp.wait()

    def gather(i):
        for h in range(H_B):
            dst = (slice(i * PAGE_SIZE, (i + 1) * PAGE_SIZE), slice(h * DV, (h + 1) * DV))
            r.kall[dst] = r.kbuf[slot, i, pl.ds(h, PAGE_SIZE, stride=H_B), :].astype(BF16)
            r.vall[dst] = r.vbuf[slot, i, pl.ds(h, PAGE_SIZE, stride=H_B), :].astype(BF16)

    def fold():
        _stream_update(r, r.kall[...], r.vall[...], tail_bias)

    return [fetch] + [functools.partial(gather, i) for i in range(cfg.ppg)] + [fold]


def _stream_finish(cfg, r):
    tdec = r.q.shape[1]
    rows_per_head = r.qbd.shape[0] // H_B
    pad = jnp.zeros((PAGE_SIZE - tdec, E_B), F32)
    _stream_update(r, jnp.concatenate([r.kn[0], pad], axis=0).astype(BF16),
                   jnp.concatenate([r.vn[0], pad], axis=0).astype(BF16), r.bias[1])
    attn = r.acc[...] * (1.0 / r.l[...])
    lam = _lam(r.lq, cfg.lam_init)
    for h in range(H_B):
        r0 = h * rows_per_head
        o = attn[r0:r0 + tdec] - lam * attn[r0 + tdec:r0 + 2 * tdec]
        r.o[0, :, h * DV:(h + 1) * DV] = _subln(o, r.g[...], cfg.lam_init).astype(r.o.dtype)


def _stream_bias(rel_table, tdec):
    tpos = np.arange(tdec)[:, None]
    key = np.arange(PAGE_SIZE)[None, :]
    last = _bias_of_distance(rel_table, PAGE_SIZE + tpos - key)
    self_ = _bias_of_distance(rel_table, np.where(key < tdec, tpos - key, -1))
    bias = jnp.stack([last, self_])
    return jnp.broadcast_to(bias[:, :, None], (2, H_B, 2, tdec, PAGE_SIZE)).reshape(2, H_B * 2 * tdec, PAGE_SIZE)


def _attn_sample_kernel(pt_ref, *refs, cfg):
    r = _StreamRefs(pt_ref, *refs)
    b = pl.program_id(0)

    @pl.when(b == 0)
    def _():
        for g in range(cfg.nbuf - 1):
            for cp in _stream_copies(cfg, r, g, g):
                cp.start()

    _stream_init(r)

    def ring_round(it, carry):
        for slot in range(cfg.nbuf):
            g = it * cfg.nbuf + slot
            tail_bias = jnp.where(g == cfg.n_groups - 1, r.bias[0], 0.0) if slot == cfg.nbuf - 1 else None
            for piece in _stream_group_pieces(cfg, r, b * cfg.n_groups + g, slot, tail_bias):
                piece()
        return carry

    lax.fori_loop(0, cfg.n_groups // cfg.nbuf, ring_round, 0)
    _stream_finish(cfg, r)


def _attn_sample(q, kn, vn, cache_k, cache_v, page_table, layer, rel_table, lq, subln_g, lam_init):
    bsz, tdec, _ = q.shape
    n_pages = page_table.shape[1]
    n_pool = cache_k.shape[1]
    assert n_pages % (PAGES_PER_GROUP * RING_SLOTS) == 0
    ncol = H_B * 2 * tdec
    assert ncol == V7X_LANES, "score rows (head, map, token) must fill one lane tile after P.V"
    cfg = _Stream(n_seq=bsz, n_groups=n_pages // PAGES_PER_GROUP, ppg=PAGES_PER_GROUP, nbuf=RING_SLOTS,
                  base=layer * n_pool, lam_init=lam_init)
    bias = _stream_bias(rel_table, tdec)
    ck = cache_k.reshape(cache_k.shape[0] * n_pool, PAGE_SIZE * H_B, DV)
    cv = cache_v.reshape(cache_v.shape[0] * n_pool, PAGE_SIZE * H_B, DV)

    per_seq = lambda b, pt: (b, 0, 0)
    ring = (cfg.nbuf, cfg.ppg, PAGE_SIZE * H_B, DV)
    grid_spec = pltpu.PrefetchScalarGridSpec(
        num_scalar_prefetch=1,
        grid=(bsz,),
        in_specs=[
            pl.BlockSpec((4, DH), lambda b, pt: (0, 0)),
            pl.BlockSpec((1, DV), lambda b, pt: (0, 0)),
            pl.BlockSpec((1, tdec, E_B), per_seq),
            pl.BlockSpec((1, tdec, E_B), per_seq),
            pl.BlockSpec((1, tdec, E_B), per_seq),
            pl.BlockSpec((2, ncol, PAGE_SIZE), lambda b, pt: (0, 0, 0)),
            pl.BlockSpec(memory_space=pl.ANY),
            pl.BlockSpec(memory_space=pl.ANY),
        ],
        out_specs=pl.BlockSpec((1, tdec, E_B), per_seq),
        scratch_shapes=[
            pltpu.VMEM((ncol, E_B), BF16),
            pltpu.VMEM(ring, F32),
            pltpu.VMEM(ring, F32),
            pltpu.VMEM((cfg.ppg * PAGE_SIZE, E_B), BF16),
            pltpu.VMEM((cfg.ppg * PAGE_SIZE, E_B), BF16),
            pltpu.VMEM((ncol, 1), F32),
            pltpu.VMEM((ncol, 1), F32),
            pltpu.VMEM((ncol, DV), F32),
            pltpu.SemaphoreType.DMA((2, cfg.nbuf)),
        ],
    )
    return pl.pallas_call(
        functools.partial(_attn_sample_kernel, cfg=cfg),
        grid_spec=grid_spec,
        out_shape=jax.ShapeDtypeStruct((bsz, tdec, E_B), BF16),
        compiler_params=pltpu.CompilerParams(
            dimension_semantics=("arbitrary",), vmem_limit_bytes=V7X_VMEM_LIMIT_BYTES),
        name="attn_sample",
    )(page_table, lq, subln_g.reshape(1, DV), q, kn, vn, bias, ck, cv)


def _post_kernel(x_ref, attn_ref, p_ref, w_u, w_v, w_za, w_zb, w_g, b_gate, gln_g, gln_b,
                 wmix, bmix, w_pa, w_pb, w_o, ln_g, ln_b, w_pe, w_pg, b_pg, *outs, alpha):
    y_ref = outs[0]
    x = x_ref[...]
    xb = x.astype(BF16)
    tm = x.shape[0]
    ck = wmix.shape[1]

    vn = _layer_norm(_gelu(_dot(xb, w_v[...])), gln_g[...], gln_b[...])
    if len(outs) > 1:
        outs[1][...] = vn
    vnb = vn.astype(BF16)
    chunks = []
    for c in range(tm // ck):
        rows = slice(c * ck, (c + 1) * ck)
        groups = [_dot(wmix[g], vnb[rows, g * C_A:(g + 1) * C_A]) + bmix[g] for g in range(G_A)]
        chunks.append(jnp.concatenate(groups, axis=1))
    mixed = jnp.concatenate(chunks, axis=0) if len(chunks) > 1 else chunks[0]
    out_a = _gelu(_dot(xb, w_u[...])) * mixed * _silu(_dot(xb, w_za[...]))
    out_b = attn_ref[...].astype(F32) * _silu(_dot(xb, w_zb[...]))
    gate = jax.nn.sigmoid(_dot(xb, w_g[...]) + b_gate[...])
    merged = (gate[:, :D_MODEL] * _dot(out_a.astype(BF16), w_pa[...])
              + gate[:, D_MODEL:] * _dot(out_b.astype(BF16), w_pb[...]))
    x1 = _layer_norm(alpha * x + _dot(merged.astype(BF16), w_o[...]), ln_g[...], ln_b[...])
    emb = _dot(p_ref[...].astype(BF16), w_pe[...])
    y_ref[...] = x1 + jax.nn.sigmoid(_dot(x1.astype(BF16), w_pg[...]) + b_pg[...]) * emb


def _post(x2d, attn2d, p2d, weights, wmix, bmix, alpha, want_v_rows):
    n = x2d.shape[0]
    tm = min(ROW_TILE, n)
    assert tm % wmix.shape[1] == 0
    row = lambda i: (i, 0)
    consts = list(weights[:8]) + [wmix, bmix] + list(weights[8:])
    out_shape = [jax.ShapeDtypeStruct((n, D_MODEL), F32)]
    out_specs = [pl.BlockSpec((tm, D_MODEL), row)]
    if want_v_rows:
        out_shape.append(jax.ShapeDtypeStruct((n, E_A), F32))
        out_specs.append(pl.BlockSpec((tm, E_A), row))
    res = pl.pallas_call(
        functools.partial(_post_kernel, alpha=alpha),
        grid=(n // tm,),
        in_specs=[pl.BlockSpec((tm, D_MODEL), row), pl.BlockSpec((tm, E_B), row),
                  pl.BlockSpec((tm, P_DIM), row)] + [_const_spec(c.shape) for c in consts],
        out_specs=out_specs,
        out_shape=out_shape,
        compiler_params=pltpu.CompilerParams(
            dimension_semantics=("arbitrary",), vmem_limit_bytes=V7X_VMEM_LIMIT_BYTES),
        name="post",
    )(x2d, attn2d, p2d, *consts)
    return res if want_v_rows else (res[0], None)


def kernel(x_prompt, x_sample, p_prompt, p_sample, cache_k, cache_v, page_table, rel_table, w_in, b_gate, gmlp_ln_g, gmlp_ln_b, w_s, b_s, lambda_qk, subln_g, w_pa, w_pb, w_o, ln_g, ln_b, w_pe, w_pg, b_pg):
    depth = w_in.shape[0]
    alpha = (2.0 * depth) ** 0.25
    bsz, seq, _ = x_prompt.shape
    dbsz, dseq, _ = x_sample.shape
    chunk = w_s.shape[-1]
    n_seq_tile = min(ROW_TILE, dbsz * dseq) // dseq
    nblk = seq // Q_TILE

    y_p = x_prompt.reshape(bsz * seq, D_MODEL)
    y_s = x_sample.reshape(dbsz * dseq, D_MODEL)
    kp_rows, vp_rows, ks_rows, vs_rows, gs_rows = [], [], [], [], []
    for l in range(depth):
        lam_init = _lambda_init(l)
        wl = w_in[l].astype(BF16)
        o = 3 * E_A
        w_qkv = wl[:, o:o + 3 * E_B]
        row2 = lambda a: a.reshape(1, -1).astype(F32)
        weights = (wl[:, 0:E_A], wl[:, E_A:2 * E_A], wl[:, 2 * E_A:3 * E_A],
                   wl[:, o + 3 * E_B:o + 4 * E_B], wl[:, o + 4 * E_B:],
                   row2(b_gate[l]), row2(gmlp_ln_g[l]), row2(gmlp_ln_b[l]),
                   w_pa[l].astype(BF16), w_pb[l].astype(BF16), w_o[l].astype(BF16),
                   row2(ln_g[l]), row2(ln_b[l]), w_pe[l].astype(BF16), w_pg[l].astype(BF16), row2(b_pg[l]))

        tril_p = jnp.tril(jnp.ones((chunk, chunk), bool))
        wmix_p = jnp.where(tril_p, w_s[l], 0.0).astype(BF16)
        bmix_p = jnp.broadcast_to(b_s[l][:, :, None], (G_A, chunk, C_A)).astype(F32)
        ws_d = jnp.where(jnp.tril(jnp.ones((dseq, dseq), bool)), w_s[l][:, :dseq, :dseq], 0.0)
        eye = jnp.eye(n_seq_tile, dtype=F32)
        wmix_s = (eye[None, :, None, :, None] * ws_d[:, None, :, None, :]).reshape(
            G_A, n_seq_tile * dseq, n_seq_tile * dseq).astype(BF16)
        bmix_s = jnp.broadcast_to(jnp.tile(b_s[l][:, :dseq], (1, n_seq_tile))[:, :, None],
                                  (G_A, n_seq_tile * dseq, C_A)).astype(F32)

        lq = lambda_qk[l].astype(F32)

        qt, kb, vt, k, v = _qkv_proj(y_p, w_qkv, True)
        attn = _attn_prompt(qt.reshape(bsz, nblk, E_B, Q_TILE), kb.reshape(bsz, seq, E_B),
                            vt.reshape(bsz, nblk, E_B, Q_TILE), rel_table, lq, subln_g[l], lam_init)
        y_p, _ = _post(y_p, attn.reshape(bsz * seq, E_B), p_prompt[l].reshape(bsz * seq, P_DIM),
                       weights, wmix_p, bmix_p, alpha, False)
        kp_rows.append(k.reshape(bsz, seq, H_B, DV))
        vp_rows.append(v.reshape(bsz, seq, H_B, DV))

        q, kn, vn, k, v = _qkv_proj(y_s, w_qkv, False)
        attn = _attn_sample(q.reshape(dbsz, dseq, E_B), kn.reshape(dbsz, dseq, E_B), vn.reshape(dbsz, dseq, E_B),
                            cache_k, cache_v, page_table, l, rel_table, lq, subln_g[l], lam_init)
        y_s, g_rows = _post(y_s, attn.reshape(dbsz * dseq, E_B), p_sample[l].reshape(dbsz * dseq, P_DIM),
                            weights, wmix_s, bmix_s, alpha, True)
        ks_rows.append(k.reshape(dbsz, dseq, H_B, DV))
        vs_rows.append(v.reshape(dbsz, dseq, H_B, DV))
        gs_rows.append(g_rows.reshape(dbsz, dseq, E_A))

    return (y_p.reshape(bsz, seq, D_MODEL), y_s.reshape(dbsz, dseq, D_MODEL),
            jnp.stack(kp_rows), jnp.stack(vp_rows), jnp.stack(ks_rows), jnp.stack(vs_rows),
            jnp.stack(gs_rows))
```

```python
import functools
import math
from typing import NamedTuple

import jax
import jax.numpy as jnp
import numpy as np
from jax import lax
from jax.experimental import pallas as pl
from jax.experimental.pallas import tpu as pltpu

F32 = jnp.float32
BF16 = jnp.bfloat16

D_MODEL = 1024
E_A = 1024
G_A = 8
C_A = E_A // G_A
H_B = 8
DH = 64
DV = 2 * DH
E_B = H_B * DV
P_DIM = 256
PAGE_SIZE = 128
NUM_BUCKETS = 32
MAX_DISTANCE = 128
LN_EPS = 1e-5
RMS_EPS = 1e-5
ATTN_SCALE = DH ** -0.5
NEG_INF = -1e30
LOG2E = math.log2(math.e)

V7X_LANES = 128
V7X_SUBLANES = 8
V7X_BF16_ROWS = 16
V7X_VMEM_LIMIT_BYTES = 60 * 1024 * 1024

Q_TILE = 256
ROW_TILE = 512
HEADS_PER_STEP = 8
PAGES_PER_GROUP = 8
RING_SLOTS = 4


def _lambda_init(layer):
    return 0.8 - 0.6 * math.exp(-0.3 * layer)


def _dot(a, b):
    return jnp.dot(a, b, preferred_element_type=F32)


def _dot_nt(a, b):
    return lax.dot_general(a, b, (((1,), (1,)), ((), ())), preferred_element_type=F32)


def _gelu(x):
    return 0.5 * x * (1.0 + lax.erf(x * (1.0 / math.sqrt(2.0))))


def _silu(x):
    return x * jax.nn.sigmoid(x)


def _layer_norm(x, g, b):
    xc = x - jnp.mean(x, axis=-1, keepdims=True)
    var = jnp.mean(xc * xc, axis=-1, keepdims=True)
    return xc * lax.rsqrt(var + LN_EPS) * g + b


def _lam(lq_ref, lam_init):
    lq = lq_ref[...]
    a = jnp.sum(lq[0:1] * lq[1:2], axis=1, keepdims=True)
    b = jnp.sum(lq[2:3] * lq[3:4], axis=1, keepdims=True)
    return jnp.exp(a) - jnp.exp(b) + lam_init


def _subln(o, g, lam_init):
    o = o * lax.rsqrt(jnp.mean(o * o, axis=-1, keepdims=True) + RMS_EPS) * g
    return o * (1.0 - lam_init)


def _const_spec(shape):
    zeros = (0,) * len(shape)
    return pl.BlockSpec(shape, lambda *_: zeros, pipeline_mode=pl.Buffered(1))


def _store_heads_major(ref, val):
    rows = val.shape[0]
    for h in range(H_B):
        ref[pl.ds(h, rows, stride=H_B), :] = val[:, h * DV:(h + 1) * DV]


def _qkv_kernel(x_ref, w_ref, q_ref, kb_ref, vb_ref, k_ref, v_ref, *, transposed):
    xb = x_ref[...].astype(BF16)
    q = _dot(xb, w_ref[:, 0:E_B]) * (ATTN_SCALE * LOG2E)
    k = _dot(xb, w_ref[:, E_B:2 * E_B])
    v = _dot(xb, w_ref[:, 2 * E_B:3 * E_B])
    _store_heads_major(k_ref, k)
    _store_heads_major(v_ref, v)
    kb_ref[...] = k.astype(kb_ref.dtype)
    if transposed:
        for j in range(q_ref.shape[0]):
            rows = slice(j * Q_TILE, (j + 1) * Q_TILE)
            q_ref[j] = q[rows].T.astype(q_ref.dtype)
            vb_ref[j] = v[rows].T.astype(vb_ref.dtype)
    else:
        q_ref[...] = q.astype(q_ref.dtype)
        vb_ref[...] = v.astype(vb_ref.dtype)


def _qkv_proj(x2d, w_qkv, transposed):
    n = x2d.shape[0]
    tm = min(ROW_TILE, n)
    row = lambda i: (i, 0)
    adt = BF16 if transposed else F32
    if transposed:
        assert tm % Q_TILE == 0
        t_shape = jax.ShapeDtypeStruct((n // Q_TILE, E_B, Q_TILE), adt)
        t_spec = pl.BlockSpec((tm // Q_TILE, E_B, Q_TILE), lambda i: (i, 0, 0))
    else:
        t_shape = jax.ShapeDtypeStruct((n, E_B), adt)
        t_spec = pl.BlockSpec((tm, E_B), row)
    return pl.pallas_call(
        functools.partial(_qkv_kernel, transposed=transposed),
        grid=(n // tm,),
        in_specs=[pl.BlockSpec((tm, D_MODEL), row), _const_spec((D_MODEL, 3 * E_B))],
        out_specs=[t_spec, pl.BlockSpec((tm, E_B), row), t_spec,
                   pl.BlockSpec((tm * H_B, DV), row), pl.BlockSpec((tm * H_B, DV), row)],
        out_shape=[t_shape, jax.ShapeDtypeStruct((n, E_B), adt), t_shape,
                   jax.ShapeDtypeStruct((n * H_B, DV), F32), jax.ShapeDtypeStruct((n * H_B, DV), F32)],
        compiler_params=pltpu.CompilerParams(
            dimension_semantics=("arbitrary",), vmem_limit_bytes=V7X_VMEM_LIMIT_BYTES),
        name="qkv_proj",
    )(x2d, w_qkv)


def _bucket_np(n):
    n = np.asarray(n)
    max_exact = NUM_BUCKETS // 2
    nf = np.maximum(n, 1).astype(np.float32)
    large = max_exact + (np.log(nf / np.float32(max_exact)) / np.float32(math.log(MAX_DISTANCE / max_exact))
                         * np.float32(NUM_BUCKETS - max_exact)).astype(np.int32)
    return np.where(n < max_exact, n, np.minimum(large, NUM_BUCKETS - 1))


def _bias_of_distance(rel_table, dist):
    dist = np.asarray(dist)
    onehot = np.eye(NUM_BUCKETS, dtype=np.float32)[_bucket_np(np.maximum(dist, 0)).reshape(-1)]
    tab = rel_table.astype(F32)
    tab = (tab - tab[NUM_BUCKETS - 1:NUM_BUCKETS]) * LOG2E
    vals = jnp.dot(jnp.asarray(onehot), tab, precision=lax.Precision.HIGHEST)
    vals = jnp.where(jnp.asarray(dist.reshape(-1, 1) >= 0), vals, NEG_INF)
    return jnp.moveaxis(vals.reshape(dist.shape + (H_B,)), -1, 0)


def _toeplitz(w, n):
    length = w.shape[1]
    a = jnp.broadcast_to(w[:, None, :], (w.shape[0], n, length))
    a = jnp.pad(a, ((0, 0), (0, 0), (0, 1))).reshape(w.shape[0], n * (length + 1))
    return a[:, :n * length].reshape(w.shape[0], n, length)


def _attn_prompt_kernel(lq_ref, g_ref, qt_ref, kb_ref, vt_ref, bias_ref, o_ref,
                        qcat_ref, m_ref, acc_ref, sa_ref, sb_ref, sd_ref, *, hps, lam_init):
    tq = qt_ref.shape[3]
    lax.fori_loop(0, qt_ref.shape[1], functools.partial(
        _attn_prompt_block, lq_ref, g_ref, qt_ref, kb_ref, vt_ref, bias_ref, o_ref,
        qcat_ref, m_ref, acc_ref, sa_ref, sb_ref, sd_ref, hps, lam_init, tq), 0)


def _attn_prompt_block(lq_ref, g_ref, qt_ref, kb_ref, vt_ref, bias_ref, o_ref,
                       qcat_ref, m_ref, acc_ref, sa_ref, sb_ref, sd_ref, hps, lam_init, tq, qi, carry):
    ones = jnp.ones((V7X_BF16_ROWS, tq), BF16)
    zero = jnp.zeros((DH, tq), BF16)

    for h in range(hps):
        qt = qt_ref[0, qi, h * DV:(h + 1) * DV, :]
        qcat_ref[h] = jnp.concatenate([jnp.concatenate([qt[:DH], zero], axis=0),
                                       jnp.concatenate([zero, qt[DH:]], axis=0)], axis=1)
        m_ref[h] = jnp.full(m_ref.shape[1:], NEG_INF, F32)
        acc_ref[h] = jnp.zeros(acc_ref.shape[1:], F32)

    def scores(j, s_ref):
        ks = pl.multiple_of(j * tq, tq)
        for h in range(hps):
            s_ref[h] = _dot(kb_ref[0, pl.ds(ks, tq), h * DV:(h + 1) * DV], qcat_ref[h])

    def update(j, s_ref, kind):
        for h in range(hps):
            s = s_ref[h]
            if kind is not None:
                b = bias_ref[h, kind]
                s = s + jnp.concatenate([b, b], axis=1)
            m_old = m_ref[h]
            m_new = jnp.maximum(m_old, jnp.max(s, axis=0, keepdims=True))
            alpha = jnp.exp2(m_old - m_new)
            p = jnp.exp2(s - m_new).astype(BF16)
            vext = jnp.concatenate([vt_ref[0, j, h * DV:(h + 1) * DV, :], ones], axis=0)
            acc_ref[h] = alpha * acc_ref[h] + _dot(vext, p)
            m_ref[h] = m_new

    n_far = jnp.maximum(qi - 1, 0)
    n_pair = n_far // 2
    scores(0, sa_ref)

    def far_pair(i, carry):
        j = 2 * i
        scores(j + 1, sb_ref)
        update(j, sa_ref, None)
        scores(j + 2, sa_ref)
        update(j + 1, sb_ref, None)
        return carry

    lax.fori_loop(0, n_pair, far_pair, 0)

    @pl.when(n_far % 2 == 1)
    def _():
        scores(qi - 1, sb_ref)
        update(qi - 2, sa_ref, None)
        scores(qi, sd_ref)
        update(qi - 1, sb_ref, 0)

    @pl.when((n_far % 2 == 0) & (qi >= 1))
    def _():
        scores(qi, sd_ref)
        update(qi - 1, sa_ref, 0)

    @pl.when(qi == 0)
    def _():
        scores(0, sd_ref)

    update(qi, sd_ref, 1)

    lam = _lam(lq_ref, lam_init)
    rows = pl.ds(pl.multiple_of(qi * tq, tq), tq)
    for h in range(hps):
        acc = acc_ref[h]
        attn = acc[:DV] * (1.0 / acc[DV:DV + 1])
        o = (attn[:, :tq] - lam * attn[:, tq:]).T
        o_ref[0, rows, h * DV:(h + 1) * DV] = _subln(o, g_ref[...], lam_init).astype(o_ref.dtype)
    return carry


def _attn_prompt(qt, kb, vt, rel_table, lq, subln_g, lam_init):
    bsz, nblk, _, tq = qt.shape
    t = nblk * tq
    hps = HEADS_PER_STEP
    w = _bias_of_distance(rel_table, np.arange(3 * tq) - tq)
    skew = _toeplitz(w, tq)
    bias = jnp.stack([skew[:, :, 2 * tq:], skew[:, :, tq:2 * tq]], axis=1)
    kernel = functools.partial(_attn_prompt_kernel, hps=hps, lam_init=lam_init)
    return pl.pallas_call(
        kernel,
        grid=(bsz, H_B // hps),
        in_specs=[
            _const_spec((4, DH)),
            _const_spec((1, DV)),
            pl.BlockSpec((1, nblk, hps * DV, tq), lambda b, g: (b, 0, g, 0)),
            pl.BlockSpec((1, t, hps * DV), lambda b, g: (b, 0, g)),
            pl.BlockSpec((1, nblk, hps * DV, tq), lambda b, g: (b, 0, g, 0)),
            pl.BlockSpec((hps, 2, tq, tq), lambda b, g: (g, 0, 0, 0)),
        ],
        out_specs=pl.BlockSpec((1, t, hps * DV), lambda b, g: (b, 0, g)),
        out_shape=jax.ShapeDtypeStruct((bsz, t, E_B), BF16),
        scratch_shapes=[
            pltpu.VMEM((hps, DV, 2 * tq), BF16),
            pltpu.VMEM((hps, 1, 2 * tq), F32),
            pltpu.VMEM((hps, DV + V7X_BF16_ROWS, 2 * tq), F32),
            pltpu.VMEM((hps, tq, 2 * tq), F32),
            pltpu.VMEM((hps, tq, 2 * tq), F32),
            pltpu.VMEM((hps, tq, 2 * tq), F32),
        ],
        compiler_params=pltpu.CompilerParams(
            dimension_semantics=("arbitrary", "arbitrary"),
            vmem_limit_bytes=V7X_VMEM_LIMIT_BYTES),
        name="attn_prompt",
    )(lq, subln_g.reshape(1, DV), qt, kb, vt, bias)


class _Stream(NamedTuple):
    n_seq: int
    n_groups: int
    ppg: int
    nbuf: int
    base: int
    lam_init: float


class _StreamRefs(NamedTuple):
    pt: object
    lq: object
    g: object
    q: object
    kn: object
    vn: object
    bias: object
    ck: object
    cv: object
    o: object
    qbd: object
    kbuf: object
    vbuf: object
    kall: object
    vall: object
    m: object
    l: object
    acc: object
    sem: object


def _stream_copies(cfg, r, lin, slot):
    seq = lax.div(lin, cfg.n_groups)
    first_page = lax.rem(lin, cfg.n_groups) * cfg.ppg
    copies = []
    for i in range(cfg.ppg):
        page = cfg.base + r.pt[seq, first_page + i]
        copies.append(pltpu.make_async_copy(r.ck.at[page], r.kbuf.at[slot, i], r.sem.at[0, slot]))
        copies.append(pltpu.make_async_copy(r.cv.at[page], r.vbuf.at[slot, i], r.sem.at[1, slot]))
    return copies


def _start_all(copies):
    for n, cp in enumerate(copies):
        cp.start(priority=n % 2)


def _stream_init(r):
    ncol, tdec = r.qbd.shape[0], r.q.shape[1]
    r.m[...] = jnp.full(r.m.shape, NEG_INF, F32)
    r.l[...] = jnp.zeros(r.l.shape, F32)
    r.acc[...] = jnp.zeros(r.acc.shape, F32)
    qrep = jnp.concatenate([r.q[0]] * (ncol // tdec), axis=0)
    row = lax.broadcasted_iota(jnp.int32, qrep.shape, 0) // tdec
    col = lax.broadcasted_iota(jnp.int32, qrep.shape, 1) // DH
    r.qbd[...] = jnp.where(row == col, qrep, 0.0).astype(BF16)


def _stream_update(r, kb, vb, tail_bias):
    rows_per_head = r.qbd.shape[0] // H_B
    s = _dot_nt(r.qbd[...], kb)
    nk = s.shape[1]
    if tail_bias is not None:
        tail = s[:, nk - PAGE_SIZE:] + tail_bias
        s = tail if nk == PAGE_SIZE else jnp.concatenate([s[:, :nk - PAGE_SIZE], tail], axis=1)
    m_old = r.m[...]
    m_new = jnp.maximum(m_old, jnp.max(s, axis=1, keepdims=True))
    alpha = jnp.exp2(m_old - m_new)
    p = jnp.exp2(s - m_new)
    r.l[...] = alpha * r.l[...] + jnp.sum(p, axis=1, keepdims=True)
    pv = _dot(p.astype(BF16), vb)
    for h in range(H_B):
        rows = slice(h * rows_per_head, (h + 1) * rows_per_head)
        r.acc[rows, :] = alpha[rows] * r.acc[rows, :] + pv[rows, h * DV:(h + 1) * DV]
    r.m[...] = m_new


def _stream_group_pieces(cfg, r, lin, slot, tail_bias):
    def fetch():
        ahead = lin + (cfg.nbuf - 1)

        @pl.when(ahead < cfg.n_seq * cfg.n_groups)
        def _():
            _start_all(_stream_copies(cfg, r, ahead, (slot + cfg.nbuf - 1) % cfg.nbuf))

        for cp in _stream_copies(cfg, r, lin, slot):
            cp.wait()

    def gather(i):
        for h in range(H_B):
            dst = (slice(i * PAGE_SIZE, (i + 1) * PAGE_SIZE), slice(h * DV, (h + 1) * DV))
            r.kall[dst] = r.kbuf[slot, i, pl.ds(h, PAGE_SIZE, stride=H_B), :].astype(BF16)
            r.vall[dst] = r.vbuf[slot, i, pl.ds(h, PAGE_SIZE, stride=H_B), :].astype(BF16)

    def fold():
        _stream_update(r, r.kall[...], r.vall[...], tail_bias)

    return [fetch] + [functools.partial(gather, i) for i in range(cfg.ppg)] + [fold]


def _stream_finish(cfg, r):
    tdec = r.q.shape[1]
    rows_per_head = r.qbd.shape[0] // H_B
    pad = jnp.zeros((PAGE_SIZE - tdec, E_B), F32)
    _stream_update(r, jnp.concatenate([r.kn[0], pad], axis=0).astype(BF16),
                   jnp.concatenate([r.vn[0], pad], axis=0).astype(BF16), r.bias[1])
    attn = r.acc[...] * (1.0 / r.l[...])
    lam = _lam(r.lq, cfg.lam_init)
    for h in range(H_B):
        r0 = h * rows_per_head
        o = attn[r0:r0 + tdec] - lam * attn[r0 + tdec:r0 + 2 * tdec]
        r.o[0, :, h * DV:(h + 1) * DV] = _subln(o, r.g[...], cfg.lam_init).astype(r.o.dtype)


def _stream_bias(rel_table, tdec):
    tpos = np.arange(tdec)[:, None]
    key = np.arange(PAGE_SIZE)[None, :]
    last = _bias_of_distance(rel_table, PAGE_SIZE + tpos - key)
    self_ = _bias_of_distance(rel_table, np.where(key < tdec, tpos - key, -1))
    bias = jnp.stack([last, self_])
    return jnp.broadcast_to(bias[:, :, None], (2, H_B, 2, tdec, PAGE_SIZE)).reshape(2, H_B * 2 * tdec, PAGE_SIZE)


def _attn_sample_kernel(pt_ref, *refs, cfg):
    r = _StreamRefs(pt_ref, *refs)
    b = pl.program_id(0)

    @pl.when(b == 0)
    def _():
        for g in range(cfg.nbuf - 1):
            _start_all(_stream_copies(cfg, r, g, g))

    _stream_init(r)

    def ring_round(it, carry):
        for slot in range(cfg.nbuf):
            g = it * cfg.nbuf + slot
            tail_bias = jnp.where(g == cfg.n_groups - 1, r.bias[0], 0.0) if slot == cfg.nbuf - 1 else None
            for piece in _stream_group_pieces(cfg, r, b * cfg.n_groups + g, slot, tail_bias):
                piece()
        return carry

    lax.fori_loop(0, cfg.n_groups // cfg.nbuf, ring_round, 0)
    _stream_finish(cfg, r)


def _attn_sample(q, kn, vn, cache_k, cache_v, page_table, layer, rel_table, lq, subln_g, lam_init):
    bsz, tdec, _ = q.shape
    n_pages = page_table.shape[1]
    n_pool = cache_k.shape[1]
    assert n_pages % (PAGES_PER_GROUP * RING_SLOTS) == 0
    ncol = H_B * 2 * tdec
    assert ncol == V7X_LANES, "score rows (head, map, token) must fill one lane tile after P.V"
    cfg = _Stream(n_seq=bsz, n_groups=n_pages // PAGES_PER_GROUP, ppg=PAGES_PER_GROUP, nbuf=RING_SLOTS,
                  base=layer * n_pool, lam_init=lam_init)
    bias = _stream_bias(rel_table, tdec)
    ck = cache_k.reshape(cache_k.shape[0] * n_pool, PAGE_SIZE * H_B, DV)
    cv = cache_v.reshape(cache_v.shape[0] * n_pool, PAGE_SIZE * H_B, DV)

    per_seq = lambda b, pt: (b, 0, 0)
    ring = (cfg.nbuf, cfg.ppg, PAGE_SIZE * H_B, DV)
    grid_spec = pltpu.PrefetchScalarGridSpec(
        num_scalar_prefetch=1,
        grid=(bsz,),
        in_specs=[
            pl.BlockSpec((4, DH), lambda b, pt: (0, 0)),
            pl.BlockSpec((1, DV), lambda b, pt: (0, 0)),
            pl.BlockSpec((1, tdec, E_B), per_seq),
            pl.BlockSpec((1, tdec, E_B), per_seq),
            pl.BlockSpec((1, tdec, E_B), per_seq),
            pl.BlockSpec((2, ncol, PAGE_SIZE), lambda b, pt: (0, 0, 0)),
            pl.BlockSpec(memory_space=pl.ANY),
            pl.BlockSpec(memory_space=pl.ANY),
        ],
        out_specs=pl.BlockSpec((1, tdec, E_B), per_seq),
        scratch_shapes=[
            pltpu.VMEM((ncol, E_B), BF16),
            pltpu.VMEM(ring, F32),
            pltpu.VMEM(ring, F32),
            pltpu.VMEM((cfg.ppg * PAGE_SIZE, E_B), BF16),
            pltpu.VMEM((cfg.ppg * PAGE_SIZE, E_B), BF16),
            pltpu.VMEM((ncol, 1), F32),
            pltpu.VMEM((ncol, 1), F32),
            pltpu.VMEM((ncol, DV), F32),
            pltpu.SemaphoreType.DMA((2, cfg.nbuf)),
        ],
    )
    return pl.pallas_call(
        functools.partial(_attn_sample_kernel, cfg=cfg),
        grid_spec=grid_spec,
        out_shape=jax.ShapeDtypeStruct((bsz, tdec, E_B), BF16),
        compiler_params=pltpu.CompilerParams(
            dimension_semantics=("arbitrary",), vmem_limit_bytes=V7X_VMEM_LIMIT_BYTES),
        name="attn_sample",
    )(page_table, lq, subln_g.reshape(1, DV), q, kn, vn, bias, ck, cv)


def _post_kernel(x_ref, attn_ref, p_ref, w_u, w_v, w_za, w_zb, w_g, b_gate, gln_g, gln_b,
                 wmix, bmix, w_pa, w_pb, w_o, ln_g, ln_b, w_pe, w_pg, b_pg, *outs, alpha):
    y_ref = outs[0]
    x = x_ref[...]
    xb = x.astype(BF16)
    tm = x.shape[0]
    ck = wmix.shape[1]

    vn = _layer_norm(_gelu(_dot(xb, w_v[...])), gln_g[...], gln_b[...])
    if len(outs) > 1:
        outs[1][...] = vn
    vnb = vn.astype(BF16)
    chunks = []
    for c in range(tm // ck):
        rows = slice(c * ck, (c + 1) * ck)
        groups = [_dot(wmix[g], vnb[rows, g * C_A:(g + 1) * C_A]) + bmix[g] for g in range(G_A)]
        chunks.append(jnp.concatenate(groups, axis=1))
    mixed = jnp.concatenate(chunks, axis=0) if len(chunks) > 1 else chunks[0]
    out_a = _gelu(_dot(xb, w_u[...])) * mixed * _silu(_dot(xb, w_za[...]))
    out_b = attn_ref[...].astype(F32) * _silu(_dot(xb, w_zb[...]))
    gate = jax.nn.sigmoid(_dot(xb, w_g[...]) + b_gate[...])
    merged = (gate[:, :D_MODEL] * _dot(out_a.astype(BF16), w_pa[...])
              + gate[:, D_MODEL:] * _dot(out_b.astype(BF16), w_pb[...]))
    x1 = _layer_norm(alpha * x + _dot(merged.astype(BF16), w_o[...]), ln_g[...], ln_b[...])
    emb = _dot(p_ref[...].astype(BF16), w_pe[...])
    y_ref[...] = x1 + jax.nn.sigmoid(_dot(x1.astype(BF16), w_pg[...]) + b_pg[...]) * emb


def _post(x2d, attn2d, p2d, weights, wmix, bmix, alpha, want_v_rows):
    n = x2d.shape[0]
    tm = min(ROW_TILE, n)
    assert tm % wmix.shape[1] == 0
    row = lambda i: (i, 0)
    consts = list(weights[:8]) + [wmix, bmix] + list(weights[8:])
    out_shape = [jax.ShapeDtypeStruct((n, D_MODEL), F32)]
    out_specs = [pl.BlockSpec((tm, D_MODEL), row)]
    if want_v_rows:
        out_shape.append(jax.ShapeDtypeStruct((n, E_A), F32))
        out_specs.append(pl.BlockSpec((tm, E_A), row))
    res = pl.pallas_call(
        functools.partial(_post_kernel, alpha=alpha),
        grid=(n // tm,),
        in_specs=[pl.BlockSpec((tm, D_MODEL), row), pl.BlockSpec((tm, E_B), row),
                  pl.BlockSpec((tm, P_DIM), row)] + [_const_spec(c.shape) for c in consts],
        out_specs=out_specs,
        out_shape=out_shape,
        compiler_params=pltpu.CompilerParams(
            dimension_semantics=("arbitrary",), vmem_limit_bytes=V7X_VMEM_LIMIT_BYTES),
        name="post",
    )(x2d, attn2d, p2d, *consts)
    return res if want_v_rows else (res[0], None)


def kernel(x_prompt, x_sample, p_prompt, p_sample, cache_k, cache_v, page_table, rel_table, w_in, b_gate, gmlp_ln_g, gmlp_ln_b, w_s, b_s, lambda_qk, subln_g, w_pa, w_pb, w_o, ln_g, ln_b, w_pe, w_pg, b_pg):
    depth = w_in.shape[0]
    alpha = (2.0 * depth) ** 0.25
    bsz, seq, _ = x_prompt.shape
    dbsz, dseq, _ = x_sample.shape
    chunk = w_s.shape[-1]
    n_seq_tile = min(ROW_TILE, dbsz * dseq) // dseq
    nblk = seq // Q_TILE

    y_p = x_prompt.reshape(bsz * seq, D_MODEL)
    y_s = x_sample.reshape(dbsz * dseq, D_MODEL)
    kp_rows, vp_rows, ks_rows, vs_rows, gs_rows = [], [], [], [], []
    for l in range(depth):
        lam_init = _lambda_init(l)
        wl = w_in[l].astype(BF16)
        o = 3 * E_A
        w_qkv = wl[:, o:o + 3 * E_B]
        row2 = lambda a: a.reshape(1, -1).astype(F32)
        weights = (wl[:, 0:E_A], wl[:, E_A:2 * E_A], wl[:, 2 * E_A:3 * E_A],
                   wl[:, o + 3 * E_B:o + 4 * E_B], wl[:, o + 4 * E_B:],
                   row2(b_gate[l]), row2(gmlp_ln_g[l]), row2(gmlp_ln_b[l]),
                   w_pa[l].astype(BF16), w_pb[l].astype(BF16), w_o[l].astype(BF16),
                   row2(ln_g[l]), row2(ln_b[l]), w_pe[l].astype(BF16), w_pg[l].astype(BF16), row2(b_pg[l]))

        tril_p = jnp.tril(jnp.ones((chunk, chunk), bool))
        wmix_p = jnp.where(tril_p, w_s[l], 0.0).astype(BF16)
        bmix_p = jnp.broadcast_to(b_s[l][:, :, None], (G_A, chunk, C_A)).astype(F32)
        ws_d = jnp.where(jnp.tril(jnp.ones((dseq, dseq), bool)), w_s[l][:, :dseq, :dseq], 0.0)
        eye = jnp.eye(n_seq_tile, dtype=F32)
        wmix_s = (eye[None, :, None, :, None] * ws_d[:, None, :, None, :]).reshape(
            G_A, n_seq_tile * dseq, n_seq_tile * dseq).astype(BF16)
        bmix_s = jnp.broadcast_to(jnp.tile(b_s[l][:, :dseq], (1, n_seq_tile))[:, :, None],
                                  (G_A, n_seq_tile * dseq, C_A)).astype(F32)

        lq = lambda_qk[l].astype(F32)

        qt, kb, vt, k, v = _qkv_proj(y_p, w_qkv, True)
        attn = _attn_prompt(qt.reshape(bsz, nblk, E_B, Q_TILE), kb.reshape(bsz, seq, E_B),
                            vt.reshape(bsz, nblk, E_B, Q_TILE), rel_table, lq, subln_g[l], lam_init)
        y_p, _ = _post(y_p, attn.reshape(bsz * seq, E_B), p_prompt[l].reshape(bsz * seq, P_DIM),
                       weights, wmix_p, bmix_p, alpha, False)
        kp_rows.append(k.reshape(bsz, seq, H_B, DV))
        vp_rows.append(v.reshape(bsz, seq, H_B, DV))

        q, kn, vn, k, v = _qkv_proj(y_s, w_qkv, False)
        attn = _attn_sample(q.reshape(dbsz, dseq, E_B), kn.reshape(dbsz, dseq, E_B), vn.reshape(dbsz, dseq, E_B),
                            cache_k, cache_v, page_table, l, rel_table, lq, subln_g[l], lam_init)
        y_s, g_rows = _post(y_s, attn.reshape(dbsz * dseq, E_B), p_sample[l].reshape(dbsz * dseq, P_DIM),
                            weights, wmix_s, bmix_s, alpha, True)
        ks_rows.append(k.reshape(dbsz, dseq, H_B, DV))
        vs_rows.append(v.reshape(dbsz, dseq, H_B, DV))
        gs_rows.append(g_rows.reshape(dbsz, dseq, E_A))

    return (y_p.reshape(bsz, seq, D_MODEL), y_s.reshape(dbsz, dseq, D_MODEL),
            jnp.stack(kp_rows), jnp.stack(vp_rows), jnp.stack(ks_rows), jnp.stack(vs_rows),
            jnp.stack(gs_rows))
```

```python
import functools
import math
from typing import NamedTuple

import jax
import jax.numpy as jnp
import numpy as np
from jax import lax
from jax.experimental import pallas as pl
from jax.experimental.pallas import tpu as pltpu

F32 = jnp.float32
BF16 = jnp.bfloat16

D_MODEL = 1024
E_A = 1024
G_A = 8
C_A = E_A // G_A
H_B = 8
DH = 64
DV = 2 * DH
E_B = H_B * DV
P_DIM = 256
PAGE_SIZE = 128
NUM_BUCKETS = 32
MAX_DISTANCE = 128
LN_EPS = 1e-5
RMS_EPS = 1e-5
ATTN_SCALE = DH ** -0.5
NEG_INF = -1e30
LOG2E = math.log2(math.e)

V7X_LANES = 128
V7X_SUBLANES = 8
V7X_BF16_ROWS = 16
V7X_VMEM_LIMIT_BYTES = 60 * 1024 * 1024

Q_TILE = 256
ROW_TILE = 512
HEADS_PER_STEP = 8
PAGES_PER_GROUP = 8
RING_SLOTS = 4


def _lambda_init(layer):
    return 0.8 - 0.6 * math.exp(-0.3 * layer)


def _dot(a, b):
    return jnp.dot(a, b, preferred_element_type=F32)


def _dot_nt(a, b):
    return lax.dot_general(a, b, (((1,), (1,)), ((), ())), preferred_element_type=F32)


def _gelu(x):
    return 0.5 * x * (1.0 + lax.erf(x * (1.0 / math.sqrt(2.0))))


def _silu(x):
    return x * jax.nn.sigmoid(x)


def _layer_norm(x, g, b):
    xc = x - jnp.mean(x, axis=-1, keepdims=True)
    var = jnp.mean(xc * xc, axis=-1, keepdims=True)
    return xc * lax.rsqrt(var + LN_EPS) * g + b


def _lam(lq_ref, lam_init):
    lq = lq_ref[...]
    a = jnp.sum(lq[0:1] * lq[1:2], axis=1, keepdims=True)
    b = jnp.sum(lq[2:3] * lq[3:4], axis=1, keepdims=True)
    return jnp.exp(a) - jnp.exp(b) + lam_init


def _subln(o, g, lam_init):
    o = o * lax.rsqrt(jnp.mean(o * o, axis=-1, keepdims=True) + RMS_EPS) * g
    return o * (1.0 - lam_init)


def _const_spec(shape):
    zeros = (0,) * len(shape)
    return pl.BlockSpec(shape, lambda *_: zeros, pipeline_mode=pl.Buffered(1))


def _cols_spec(rows, start, width):
    assert start % width == 0
    return pl.BlockSpec((rows, width), lambda *_: (0, start // width), pipeline_mode=pl.Buffered(1))


def _store_heads_major(ref, val):
    rows = val.shape[0]
    for h in range(H_B):
        ref[pl.ds(h, rows, stride=H_B), :] = val[:, h * DV:(h + 1) * DV]


def _qkv_kernel(x_ref, w_ref, q_ref, kb_ref, vb_ref, k_ref, v_ref, *, transposed):
    xb = x_ref[...].astype(BF16)
    q = _dot(xb, w_ref[:, 0:E_B]) * (ATTN_SCALE * LOG2E)
    k = _dot(xb, w_ref[:, E_B:2 * E_B])
    v = _dot(xb, w_ref[:, 2 * E_B:3 * E_B])
    _store_heads_major(k_ref, k)
    _store_heads_major(v_ref, v)
    kb_ref[...] = k.astype(kb_ref.dtype)
    if transposed:
        for j in range(q_ref.shape[0]):
            rows = slice(j * Q_TILE, (j + 1) * Q_TILE)
            q_ref[j] = q[rows].T.astype(q_ref.dtype)
            vb_ref[j] = v[rows].T.astype(vb_ref.dtype)
    else:
        q_ref[...] = q.astype(q_ref.dtype)
        vb_ref[...] = v.astype(vb_ref.dtype)


def _qkv_proj(x2d, w_in, transposed):
    n = x2d.shape[0]
    tm = min(ROW_TILE, n)
    assert n % tm == 0
    row = lambda i: (i, 0)
    adt = BF16 if transposed else F32
    if transposed:
        assert tm % Q_TILE == 0
        t_shape = jax.ShapeDtypeStruct((n // Q_TILE, E_B, Q_TILE), adt)
        t_spec = pl.BlockSpec((tm // Q_TILE, E_B, Q_TILE), lambda i: (i, 0, 0))
    else:
        t_shape = jax.ShapeDtypeStruct((n, E_B), adt)
        t_spec = pl.BlockSpec((tm, E_B), row)
    return pl.pallas_call(
        functools.partial(_qkv_kernel, transposed=transposed),
        grid=(n // tm,),
        in_specs=[pl.BlockSpec((tm, D_MODEL), row), _cols_spec(D_MODEL, 3 * E_A, 3 * E_B)],
        out_specs=[t_spec, pl.BlockSpec((tm, E_B), row), t_spec,
                   pl.BlockSpec((tm * H_B, DV), row), pl.BlockSpec((tm * H_B, DV), row)],
        out_shape=[t_shape, jax.ShapeDtypeStruct((n, E_B), adt), t_shape,
                   jax.ShapeDtypeStruct((n * H_B, DV), F32), jax.ShapeDtypeStruct((n * H_B, DV), F32)],
        compiler_params=pltpu.CompilerParams(
            dimension_semantics=("arbitrary",), vmem_limit_bytes=V7X_VMEM_LIMIT_BYTES),
        name="qkv_proj",
    )(x2d, w_in)


def _bucket_np(n):
    n = np.asarray(n)
    max_exact = NUM_BUCKETS // 2
    nf = np.maximum(n, 1).astype(np.float32)
    large = max_exact + (np.log(nf / np.float32(max_exact)) / np.float32(math.log(MAX_DISTANCE / max_exact))
                         * np.float32(NUM_BUCKETS - max_exact)).astype(np.int32)
    return np.where(n < max_exact, n, np.minimum(large, NUM_BUCKETS - 1))


def _bias_of_distance(rel_table, dist):
    dist = np.asarray(dist)
    onehot = np.eye(NUM_BUCKETS, dtype=np.float32)[_bucket_np(np.maximum(dist, 0)).reshape(-1)]
    tab = rel_table.astype(F32)
    tab = (tab - tab[NUM_BUCKETS - 1:NUM_BUCKETS]) * LOG2E
    vals = jnp.dot(jnp.asarray(onehot), tab, precision=lax.Precision.HIGHEST)
    vals = jnp.where(jnp.asarray(dist.reshape(-1, 1) >= 0), vals, NEG_INF)
    return jnp.moveaxis(vals.reshape(dist.shape + (H_B,)), -1, 0)


def _toeplitz(w, n):
    length = w.shape[1]
    a = jnp.broadcast_to(w[:, None, :], (w.shape[0], n, length))
    a = jnp.pad(a, ((0, 0), (0, 0), (0, 1))).reshape(w.shape[0], n * (length + 1))
    return a[:, :n * length].reshape(w.shape[0], n, length)


def _attn_prompt_kernel(lq_ref, g_ref, qt_ref, kb_ref, vt_ref, bias_ref, o_ref,
                        qcat_ref, m_ref, acc_ref, sa_ref, sb_ref, sd_ref, *, hps, lam_init):
    tq = qt_ref.shape[3]
    lax.fori_loop(0, qt_ref.shape[1], functools.partial(
        _attn_prompt_block, lq_ref, g_ref, qt_ref, kb_ref, vt_ref, bias_ref, o_ref,
        qcat_ref, m_ref, acc_ref, sa_ref, sb_ref, sd_ref, hps, lam_init, tq), 0)


def _attn_prompt_block(lq_ref, g_ref, qt_ref, kb_ref, vt_ref, bias_ref, o_ref,
                       qcat_ref, m_ref, acc_ref, sa_ref, sb_ref, sd_ref, hps, lam_init, tq, qi, carry):
    ones = jnp.ones((V7X_BF16_ROWS, tq), BF16)
    zero = jnp.zeros((DH, tq), BF16)

    for h in range(hps):
        qt = qt_ref[0, qi, h * DV:(h + 1) * DV, :]
        qcat_ref[h] = jnp.concatenate([jnp.concatenate([qt[:DH], zero], axis=0),
                                       jnp.concatenate([zero, qt[DH:]], axis=0)], axis=1)
        m_ref[h] = jnp.full(m_ref.shape[1:], NEG_INF, F32)
        acc_ref[h] = jnp.zeros(acc_ref.shape[1:], F32)

    def scores(j, s_ref):
        ks = pl.multiple_of(j * tq, tq)
        for h in range(hps):
            s_ref[h] = _dot(kb_ref[0, pl.ds(ks, tq), h * DV:(h + 1) * DV], qcat_ref[h])

    def update(j, s_ref, kind):
        for h in range(hps):
            s = s_ref[h]
            if kind is not None:
                b = bias_ref[h, kind]
                s = s + jnp.concatenate([b, b], axis=1)
            m_old = m_ref[h]
            m_new = jnp.maximum(m_old, jnp.max(s, axis=0, keepdims=True))
            alpha = jnp.exp2(m_old - m_new)
            p = jnp.exp2(s - m_new).astype(BF16)
            vext = jnp.concatenate([vt_ref[0, j, h * DV:(h + 1) * DV, :], ones], axis=0)
            acc_ref[h] = alpha * acc_ref[h] + _dot(vext, p)
            m_ref[h] = m_new

    n_far = jnp.maximum(qi - 1, 0)
    n_pair = n_far // 2
    scores(0, sa_ref)

    def far_pair(i, carry):
        j = 2 * i
        scores(j + 1, sb_ref)
        update(j, sa_ref, None)
        scores(j + 2, sa_ref)
        update(j + 1, sb_ref, None)
        return carry

    lax.fori_loop(0, n_pair, far_pair, 0)

    @pl.when(n_far % 2 == 1)
    def _():
        scores(qi - 1, sb_ref)
        update(qi - 2, sa_ref, None)
        scores(qi, sd_ref)
        update(qi - 1, sb_ref, 0)

    @pl.when((n_far % 2 == 0) & (qi >= 1))
    def _():
        scores(qi, sd_ref)
        update(qi - 1, sa_ref, 0)

    @pl.when(qi == 0)
    def _():
        scores(0, sd_ref)

    update(qi, sd_ref, 1)

    lam = _lam(lq_ref, lam_init)
    rows = pl.ds(pl.multiple_of(qi * tq, tq), tq)
    for h in range(hps):
        acc = acc_ref[h]
        attn = acc[:DV] * (1.0 / acc[DV:DV + 1])
        o = (attn[:, :tq] - lam * attn[:, tq:]).T
        o_ref[0, rows, h * DV:(h + 1) * DV] = _subln(o, g_ref[...], lam_init).astype(o_ref.dtype)
    return carry


def _attn_prompt(qt, kb, vt, rel_table, lq, subln_g, lam_init):
    bsz, nblk, _, tq = qt.shape
    t = nblk * tq
    hps = HEADS_PER_STEP
    w = _bias_of_distance(rel_table, np.arange(3 * tq) - tq)
    skew = _toeplitz(w, tq)
    bias = jnp.stack([skew[:, :, 2 * tq:], skew[:, :, tq:2 * tq]], axis=1)
    kernel = functools.partial(_attn_prompt_kernel, hps=hps, lam_init=lam_init)
    return pl.pallas_call(
        kernel,
        grid=(bsz, H_B // hps),
        in_specs=[
            _const_spec((4, DH)),
            _const_spec((1, DV)),
            pl.BlockSpec((1, nblk, hps * DV, tq), lambda b, g: (b, 0, g, 0)),
            pl.BlockSpec((1, t, hps * DV), lambda b, g: (b, 0, g)),
            pl.BlockSpec((1, nblk, hps * DV, tq), lambda b, g: (b, 0, g, 0)),
            pl.BlockSpec((hps, 2, tq, tq), lambda b, g: (g, 0, 0, 0)),
        ],
        out_specs=pl.BlockSpec((1, t, hps * DV), lambda b, g: (b, 0, g)),
        out_shape=jax.ShapeDtypeStruct((bsz, t, E_B), BF16),
        scratch_shapes=[
            pltpu.VMEM((hps, DV, 2 * tq), BF16),
            pltpu.VMEM((hps, 1, 2 * tq), F32),
            pltpu.VMEM((hps, DV + V7X_BF16_ROWS, 2 * tq), F32),
            pltpu.VMEM((hps, tq, 2 * tq), F32),
            pltpu.VMEM((hps, tq, 2 * tq), F32),
            pltpu.VMEM((hps, tq, 2 * tq), F32),
        ],
        compiler_params=pltpu.CompilerParams(
            dimension_semantics=("arbitrary", "arbitrary"),
            vmem_limit_bytes=V7X_VMEM_LIMIT_BYTES),
        name="attn_prompt",
    )(lq, subln_g.reshape(1, DV), qt, kb, vt, bias)


class _Stream(NamedTuple):
    n_seq: int
    n_groups: int
    ppg: int
    nbuf: int
    base: int
    lam_init: float


class _StreamRefs(NamedTuple):
    pt: object
    lq: object
    g: object
    q: object
    kn: object
    vn: object
    bias: object
    ck: object
    cv: object
    o: object
    qbd: object
    kbuf: object
    vbuf: object
    kall: object
    vall: object
    m: object
    l: object
    acc: object
    sem: object


def _stream_copies(cfg, r, lin, slot):
    seq = lax.div(lin, cfg.n_groups)
    first_page = lax.rem(lin, cfg.n_groups) * cfg.ppg
    copies = []
    for i in range(cfg.ppg):
        page = cfg.base + r.pt[seq, first_page + i]
        copies.append(pltpu.make_async_copy(r.ck.at[page], r.kbuf.at[slot, i], r.sem.at[0, slot]))
        copies.append(pltpu.make_async_copy(r.cv.at[page], r.vbuf.at[slot, i], r.sem.at[1, slot]))
    return copies


def _start_all(copies):
    for n, cp in enumerate(copies):
        cp.start(priority=n % 2)


def _stream_init(r):
    ncol, tdec = r.qbd.shape[0], r.q.shape[1]
    r.m[...] = jnp.full(r.m.shape, NEG_INF, F32)
    r.l[...] = jnp.zeros(r.l.shape, F32)
    r.acc[...] = jnp.zeros(r.acc.shape, F32)
    qrep = jnp.concatenate([r.q[0]] * (ncol // tdec), axis=0)
    row = lax.broadcasted_iota(jnp.int32, qrep.shape, 0) // tdec
    col = lax.broadcasted_iota(jnp.int32, qrep.shape, 1) // DH
    r.qbd[...] = jnp.where(row == col, qrep, 0.0).astype(BF16)


def _stream_update(r, kb, vb, tail_bias):
    rows_per_head = r.qbd.shape[0] // H_B
    s = _dot_nt(r.qbd[...], kb)
    nk = s.shape[1]
    if tail_bias is not None:
        tail = s[:, nk - PAGE_SIZE:] + tail_bias
        s = tail if nk == PAGE_SIZE else jnp.concatenate([s[:, :nk - PAGE_SIZE], tail], axis=1)
    m_old = r.m[...]
    m_new = jnp.maximum(m_old, jnp.max(s, axis=1, keepdims=True))
    alpha = jnp.exp2(m_old - m_new)
    p = jnp.exp2(s - m_new)
    r.l[...] = alpha * r.l[...] + jnp.sum(p, axis=1, keepdims=True)
    pv = _dot(p.astype(BF16), vb)
    for h in range(H_B):
        rows = slice(h * rows_per_head, (h + 1) * rows_per_head)
        r.acc[rows, :] = alpha[rows] * r.acc[rows, :] + pv[rows, h * DV:(h + 1) * DV]
    r.m[...] = m_new


def _stream_group_pieces(cfg, r, lin, slot, tail_bias):
    def fetch():
        ahead = lin + (cfg.nbuf - 1)

        @pl.when(ahead < cfg.n_seq * cfg.n_groups)
        def _():
            _start_all(_stream_copies(cfg, r, ahead, (slot + cfg.nbuf - 1) % cfg.nbuf))

        for cp in _stream_copies(cfg, r, lin, slot):
            cp.wait()

    def gather(i):
        for h in range(H_B):
            dst = (slice(i * PAGE_SIZE, (i + 1) * PAGE_SIZE), slice(h * DV, (h + 1) * DV))
            r.kall[dst] = r.kbuf[slot, i, pl.ds(h, PAGE_SIZE, stride=H_B), :].astype(BF16)
            r.vall[dst] = r.vbuf[slot, i, pl.ds(h, PAGE_SIZE, stride=H_B), :].astype(BF16)

    def fold():
        _stream_update(r, r.kall[...], r.vall[...], tail_bias)

    return [fetch] + [functools.partial(gather, i) for i in range(cfg.ppg)] + [fold]


def _stream_finish(cfg, r):
    tdec = r.q.shape[1]
    rows_per_head = r.qbd.shape[0] // H_B
    pad = jnp.zeros((PAGE_SIZE - tdec, E_B), F32)
    _stream_update(r, jnp.concatenate([r.kn[0], pad], axis=0).astype(BF16),
                   jnp.concatenate([r.vn[0], pad], axis=0).astype(BF16), r.bias[1])
    attn = r.acc[...] * (1.0 / r.l[...])
    lam = _lam(r.lq, cfg.lam_init)
    for h in range(H_B):
        r0 = h * rows_per_head
        o = attn[r0:r0 + tdec] - lam * attn[r0 + tdec:r0 + 2 * tdec]
        r.o[0, :, h * DV:(h + 1) * DV] = _subln(o, r.g[...], cfg.lam_init).astype(r.o.dtype)


def _stream_bias(rel_table, tdec):
    tpos = np.arange(tdec)[:, None]
    key = np.arange(PAGE_SIZE)[None, :]
    last = _bias_of_distance(rel_table, PAGE_SIZE + tpos - key)
    self_ = _bias_of_distance(rel_table, np.where(key < tdec, tpos - key, -1))
    bias = jnp.stack([last, self_])
    return jnp.broadcast_to(bias[:, :, None], (2, H_B, 2, tdec, PAGE_SIZE)).reshape(2, H_B * 2 * tdec, PAGE_SIZE)


def _attn_sample_kernel(pt_ref, *refs, cfg):
    r = _StreamRefs(pt_ref, *refs)
    b = pl.program_id(0)

    @pl.when(b == 0)
    def _():
        for g in range(cfg.nbuf - 1):
            _start_all(_stream_copies(cfg, r, g, g))

    _stream_init(r)

    def ring_round(it, carry):
        for slot in range(cfg.nbuf):
            g = it * cfg.nbuf + slot
            tail_bias = jnp.where(g == cfg.n_groups - 1, r.bias[0], 0.0) if slot == cfg.nbuf - 1 else None
            for piece in _stream_group_pieces(cfg, r, b * cfg.n_groups + g, slot, tail_bias):
                piece()
        return carry

    lax.fori_loop(0, cfg.n_groups // cfg.nbuf, ring_round, 0)
    _stream_finish(cfg, r)


def _attn_sample(q, kn, vn, cache_k, cache_v, page_table, layer, rel_table, lq, subln_g, lam_init):
    bsz, tdec, _ = q.shape
    n_pages = page_table.shape[1]
    n_pool = cache_k.shape[1]
    assert n_pages % (PAGES_PER_GROUP * RING_SLOTS) == 0
    ncol = H_B * 2 * tdec
    assert ncol == V7X_LANES, "score rows (head, map, token) must fill one lane tile after P.V"
    cfg = _Stream(n_seq=bsz, n_groups=n_pages // PAGES_PER_GROUP, ppg=PAGES_PER_GROUP, nbuf=RING_SLOTS,
                  base=layer * n_pool, lam_init=lam_init)
    bias = _stream_bias(rel_table, tdec)
    ck = cache_k.reshape(cache_k.shape[0] * n_pool, PAGE_SIZE * H_B, DV)
    cv = cache_v.reshape(cache_v.shape[0] * n_pool, PAGE_SIZE * H_B, DV)

    per_seq = lambda b, pt: (b, 0, 0)
    ring = (cfg.nbuf, cfg.ppg, PAGE_SIZE * H_B, DV)
    grid_spec = pltpu.PrefetchScalarGridSpec(
        num_scalar_prefetch=1,
        grid=(bsz,),
        in_specs=[
            pl.BlockSpec((4, DH), lambda b, pt: (0, 0)),
            pl.BlockSpec((1, DV), lambda b, pt: (0, 0)),
            pl.BlockSpec((1, tdec, E_B), per_seq),
            pl.BlockSpec((1, tdec, E_B), per_seq),
            pl.BlockSpec((1, tdec, E_B), per_seq),
            pl.BlockSpec((2, ncol, PAGE_SIZE), lambda b, pt: (0, 0, 0)),
            pl.BlockSpec(memory_space=pl.ANY),
            pl.BlockSpec(memory_space=pl.ANY),
        ],
        out_specs=pl.BlockSpec((1, tdec, E_B), per_seq),
        scratch_shapes=[
            pltpu.VMEM((ncol, E_B), BF16),
            pltpu.VMEM(ring, F32),
            pltpu.VMEM(ring, F32),
            pltpu.VMEM((cfg.ppg * PAGE_SIZE, E_B), BF16),
            pltpu.VMEM((cfg.ppg * PAGE_SIZE, E_B), BF16),
            pltpu.VMEM((ncol, 1), F32),
            pltpu.VMEM((ncol, 1), F32),
            pltpu.VMEM((ncol, DV), F32),
            pltpu.SemaphoreType.DMA((2, cfg.nbuf)),
        ],
    )
    return pl.pallas_call(
        functools.partial(_attn_sample_kernel, cfg=cfg),
        grid_spec=grid_spec,
        out_shape=jax.ShapeDtypeStruct((bsz, tdec, E_B), BF16),
        compiler_params=pltpu.CompilerParams(
            dimension_semantics=("arbitrary",), vmem_limit_bytes=V7X_VMEM_LIMIT_BYTES),
        name="attn_sample",
    )(page_table, lq, subln_g.reshape(1, DV), q, kn, vn, bias, ck, cv)


def _post_kernel(x_ref, attn_ref, p_ref, w_u, w_v, w_za, w_zb, w_ga, w_gb, b_gate, gln_g, gln_b,
                 wmix, bmix, w_pa, w_pb, w_o, ln_g, ln_b, w_pe, w_pg, b_pg, *outs, alpha):
    y_ref = outs[0]
    x = x_ref[...]
    xb = x.astype(BF16)
    tm = x.shape[0]
    ck = wmix.shape[1]

    vn = _layer_norm(_gelu(_dot(xb, w_v[...])), gln_g[...], gln_b[...])
    if len(outs) > 1:
        outs[1][...] = vn
    vnb = vn.astype(BF16)
    chunks = []
    for c in range(tm // ck):
        rows = slice(c * ck, (c + 1) * ck)
        groups = [_dot(wmix[g], vnb[rows, g * C_A:(g + 1) * C_A]) + bmix[g] for g in range(G_A)]
        chunks.append(jnp.concatenate(groups, axis=1))
    mixed = jnp.concatenate(chunks, axis=0) if len(chunks) > 1 else chunks[0]
    out_a = _gelu(_dot(xb, w_u[...])) * mixed * _silu(_dot(xb, w_za[...]))
    out_b = attn_ref[...].astype(F32) * _silu(_dot(xb, w_zb[...]))
    gate_a = jax.nn.sigmoid(_dot(xb, w_ga[...]) + b_gate[:, :D_MODEL])
    gate_b = jax.nn.sigmoid(_dot(xb, w_gb[...]) + b_gate[:, D_MODEL:])
    merged = (gate_a * _dot(out_a.astype(BF16), w_pa[...])
              + gate_b * _dot(out_b.astype(BF16), w_pb[...]))
    x1 = _layer_norm(alpha * x + _dot(merged.astype(BF16), w_o[...]), ln_g[...], ln_b[...])
    emb = _dot(p_ref[...].astype(BF16), w_pe[...])
    y_ref[...] = x1 + jax.nn.sigmoid(_dot(x1.astype(BF16), w_pg[...]) + b_pg[...]) * emb


def _post(x2d, attn2d, p2d, w_in, weights, wmix, bmix, alpha, want_v_rows):
    n = x2d.shape[0]
    tm = min(ROW_TILE, n)
    assert n % tm == 0 and tm % wmix.shape[1] == 0
    row = lambda i: (i, 0)
    in_cols = [0, E_A, 2 * E_A, 3 * E_A + 3 * E_B, 3 * E_A + 4 * E_B, 3 * E_A + 4 * E_B + D_MODEL]
    consts = list(weights[:3]) + [wmix, bmix] + list(weights[3:])
    out_shape = [jax.ShapeDtypeStruct((n, D_MODEL), F32)]
    out_specs = [pl.BlockSpec((tm, D_MODEL), row)]
    if want_v_rows:
        out_shape.append(jax.ShapeDtypeStruct((n, E_A), F32))
        out_specs.append(pl.BlockSpec((tm, E_A), row))
    res = pl.pallas_call(
        functools.partial(_post_kernel, alpha=alpha),
        grid=(n // tm,),
        in_specs=[pl.BlockSpec((tm, D_MODEL), row), pl.BlockSpec((tm, E_B), row),
                  pl.BlockSpec((tm, P_DIM), row)]
                 + [_cols_spec(D_MODEL, c, D_MODEL) for c in in_cols] + [_const_spec(c.shape) for c in consts],
        out_specs=out_specs,
        out_shape=out_shape,
        compiler_params=pltpu.CompilerParams(
            dimension_semantics=("arbitrary",), vmem_limit_bytes=V7X_VMEM_LIMIT_BYTES),
        name="post",
    )(x2d, attn2d, p2d, *([w_in] * len(in_cols)), *consts)
    return res if want_v_rows else (res[0], None)


def kernel(x_prompt, x_sample, p_prompt, p_sample, cache_k, cache_v, page_table, rel_table, w_in, b_gate, gmlp_ln_g, gmlp_ln_b, w_s, b_s, lambda_qk, subln_g, w_pa, w_pb, w_o, ln_g, ln_b, w_pe, w_pg, b_pg):
    depth = w_in.shape[0]
    alpha = (2.0 * depth) ** 0.25
    bsz, seq, _ = x_prompt.shape
    dbsz, dseq, _ = x_sample.shape
    chunk = w_s.shape[-1]
    n_seq_tile = min(ROW_TILE, dbsz * dseq) // dseq
    nblk = seq // Q_TILE

    y_p = x_prompt.reshape(bsz * seq, D_MODEL)
    y_s = x_sample.reshape(dbsz * dseq, D_MODEL)
    kp_rows, vp_rows, ks_rows, vs_rows, gs_rows = [], [], [], [], []
    for l in range(depth):
        lam_init = _lambda_init(l)
        wl = w_in[l].astype(BF16)
        row2 = lambda a: a.reshape(1, -1).astype(F32)
        weights = (row2(b_gate[l]), row2(gmlp_ln_g[l]), row2(gmlp_ln_b[l]),
                   w_pa[l].astype(BF16), w_pb[l].astype(BF16), w_o[l].astype(BF16),
                   row2(ln_g[l]), row2(ln_b[l]), w_pe[l].astype(BF16), w_pg[l].astype(BF16), row2(b_pg[l]))

        tril_p = jnp.tril(jnp.ones((chunk, chunk), bool))
        wmix_p = jnp.where(tril_p, w_s[l], 0.0).astype(BF16)
        bmix_p = jnp.broadcast_to(b_s[l][:, :, None], (G_A, chunk, C_A)).astype(F32)
        ws_d = jnp.where(jnp.tril(jnp.ones((dseq, dseq), bool)), w_s[l][:, :dseq, :dseq], 0.0)
        rep = jnp.asarray(np.tile(np.eye(dseq, dtype=np.float32), (n_seq_tile, 1)))
        same_seq = np.kron(np.eye(n_seq_tile), np.ones((dseq, dseq))) > 0
        tiled = jnp.einsum('ia,gab,jb->gij', rep, ws_d, rep, precision=lax.Precision.HIGHEST)
        wmix_s = jnp.where(jnp.asarray(same_seq), tiled, 0.0).astype(BF16)
        bmix_s = jnp.broadcast_to(jnp.tile(b_s[l][:, :dseq], (1, n_seq_tile))[:, :, None],
                                  (G_A, n_seq_tile * dseq, C_A)).astype(F32)

        lq = lambda_qk[l].astype(F32)

        qt, kb, vt, k, v = _qkv_proj(y_p, wl, True)
        attn = _attn_prompt(qt.reshape(bsz, nblk, E_B, Q_TILE), kb.reshape(bsz, seq, E_B),
                            vt.reshape(bsz, nblk, E_B, Q_TILE), rel_table, lq, subln_g[l], lam_init)
        y_p, _ = _post(y_p, attn.reshape(bsz * seq, E_B), p_prompt[l].reshape(bsz * seq, P_DIM),
                       wl, weights, wmix_p, bmix_p, alpha, False)
        kp_rows.append(k.reshape(bsz, seq, H_B, DV))
        vp_rows.append(v.reshape(bsz, seq, H_B, DV))

        q, kn, vn, k, v = _qkv_proj(y_s, wl, False)
        attn = _attn_sample(q.reshape(dbsz, dseq, E_B), kn.reshape(dbsz, dseq, E_B), vn.reshape(dbsz, dseq, E_B),
                            cache_k, cache_v, page_table, l, rel_table, lq, subln_g[l], lam_init)
        y_s, g_rows = _post(y_s, attn.reshape(dbsz * dseq, E_B), p_sample[l].reshape(dbsz * dseq, P_DIM),
                            wl, weights, wmix_s, bmix_s, alpha, True)
        ks_rows.append(k.reshape(dbsz, dseq, H_B, DV))
        vs_rows.append(v.reshape(dbsz, dseq, H_B, DV))
        gs_rows.append(g_rows.reshape(dbsz, dseq, E_A))

    return (y_p.reshape(bsz, seq, D_MODEL), y_s.reshape(dbsz, dseq, D_MODEL),
            jnp.stack(kp_rows), jnp.stack(vp_rows), jnp.stack(ks_rows), jnp.stack(vs_rows),
            jnp.stack(gs_rows))
```

```python
import functools
import math
from typing import NamedTuple

import jax
import jax.numpy as jnp
import numpy as np
from jax import lax
from jax.experimental import pallas as pl
from jax.experimental.pallas import tpu as pltpu

F32 = jnp.float32
BF16 = jnp.bfloat16

D_MODEL = 1024
E_A = 1024
G_A = 8
C_A = E_A // G_A
H_B = 8
DH = 64
DV = 2 * DH
E_B = H_B * DV
P_DIM = 256
PAGE_SIZE = 128
HEAD_PAIRS = H_B // 2
VKEYS = 2 * PAGE_SIZE
NUM_BUCKETS = 32
MAX_DISTANCE = 128
LN_EPS = 1e-5
RMS_EPS = 1e-5
ATTN_SCALE = DH ** -0.5
NEG_INF = -1e30
LOG2E = math.log2(math.e)

V7X_LANES = 128
V7X_SUBLANES = 8
V7X_BF16_ROWS = 16
V7X_VMEM_LIMIT_BYTES = 60 * 1024 * 1024

Q_TILE = 256
ROW_TILE = 512
HEADS_PER_STEP = 8
PAGES_PER_GROUP = 8
RING_SLOTS = 4


def _lambda_init(layer):
    return 0.8 - 0.6 * math.exp(-0.3 * layer)


def _dot(a, b):
    return jnp.dot(a, b, preferred_element_type=F32)


def _dot_nt(a, b):
    return lax.dot_general(a, b, (((1,), (1,)), ((), ())), preferred_element_type=F32)


def _gelu(x):
    return 0.5 * x * (1.0 + lax.erf(x * (1.0 / math.sqrt(2.0))))


def _silu(x):
    return x * jax.nn.sigmoid(x)


def _layer_norm(x, g, b):
    xc = x - jnp.mean(x, axis=-1, keepdims=True)
    var = jnp.mean(xc * xc, axis=-1, keepdims=True)
    return xc * lax.rsqrt(var + LN_EPS) * g + b


def _lam(lq_ref, lam_init):
    lq = lq_ref[...]
    a = jnp.sum(lq[0:1] * lq[1:2], axis=1, keepdims=True)
    b = jnp.sum(lq[2:3] * lq[3:4], axis=1, keepdims=True)
    return jnp.exp(a) - jnp.exp(b) + lam_init


def _subln(o, g, lam_init):
    o = o * lax.rsqrt(jnp.mean(o * o, axis=-1, keepdims=True) + RMS_EPS) * g
    return o * (1.0 - lam_init)


def _const_spec(shape):
    zeros = (0,) * len(shape)
    return pl.BlockSpec(shape, lambda *_: zeros, pipeline_mode=pl.Buffered(1))


def _cols_spec(rows, start, width):
    assert start % width == 0
    return pl.BlockSpec((rows, width), lambda *_: (0, start // width), pipeline_mode=pl.Buffered(1))


def _store_heads_major(ref, val):
    rows = val.shape[0]
    for h in range(H_B):
        ref[pl.ds(h, rows, stride=H_B), :] = val[:, h * DV:(h + 1) * DV]


def _qkv_kernel(x_ref, w_ref, q_ref, kb_ref, vb_ref, k_ref, v_ref, *, transposed):
    xb = x_ref[...].astype(BF16)
    q = _dot(xb, w_ref[:, 0:E_B]) * (ATTN_SCALE * LOG2E)
    k = _dot(xb, w_ref[:, E_B:2 * E_B])
    v = _dot(xb, w_ref[:, 2 * E_B:3 * E_B])
    _store_heads_major(k_ref, k)
    _store_heads_major(v_ref, v)
    kb_ref[...] = k.astype(kb_ref.dtype)
    if transposed:
        for j in range(q_ref.shape[0]):
            rows = slice(j * Q_TILE, (j + 1) * Q_TILE)
            q_ref[j] = q[rows].T.astype(q_ref.dtype)
            vb_ref[j] = v[rows].T.astype(vb_ref.dtype)
    else:
        q_ref[...] = q.astype(q_ref.dtype)
        vb_ref[...] = v.astype(vb_ref.dtype)


def _qkv_proj(x2d, w_in, transposed):
    n = x2d.shape[0]
    tm = min(ROW_TILE, n)
    assert n % tm == 0
    row = lambda i: (i, 0)
    adt = BF16 if transposed else F32
    if transposed:
        assert tm % Q_TILE == 0
        t_shape = jax.ShapeDtypeStruct((n // Q_TILE, E_B, Q_TILE), adt)
        t_spec = pl.BlockSpec((tm // Q_TILE, E_B, Q_TILE), lambda i: (i, 0, 0))
    else:
        t_shape = jax.ShapeDtypeStruct((n, E_B), adt)
        t_spec = pl.BlockSpec((tm, E_B), row)
    return pl.pallas_call(
        functools.partial(_qkv_kernel, transposed=transposed),
        grid=(n // tm,),
        in_specs=[pl.BlockSpec((tm, D_MODEL), row), _cols_spec(D_MODEL, 3 * E_A, 3 * E_B)],
        out_specs=[t_spec, pl.BlockSpec((tm, E_B), row), t_spec,
                   pl.BlockSpec((tm * H_B, DV), row), pl.BlockSpec((tm * H_B, DV), row)],
        out_shape=[t_shape, jax.ShapeDtypeStruct((n, E_B), adt), t_shape,
                   jax.ShapeDtypeStruct((n * H_B, DV), F32), jax.ShapeDtypeStruct((n * H_B, DV), F32)],
        compiler_params=pltpu.CompilerParams(
            dimension_semantics=("arbitrary",), vmem_limit_bytes=V7X_VMEM_LIMIT_BYTES),
        name="qkv_proj",
    )(x2d, w_in)


def _bucket_np(n):
    n = np.asarray(n)
    max_exact = NUM_BUCKETS // 2
    nf = np.maximum(n, 1).astype(np.float32)
    large = max_exact + (np.log(nf / np.float32(max_exact)) / np.float32(math.log(MAX_DISTANCE / max_exact))
                         * np.float32(NUM_BUCKETS - max_exact)).astype(np.int32)
    return np.where(n < max_exact, n, np.minimum(large, NUM_BUCKETS - 1))


def _bias_of_distance(rel_table, dist):
    dist = np.asarray(dist)
    onehot = np.eye(NUM_BUCKETS, dtype=np.float32)[_bucket_np(np.maximum(dist, 0)).reshape(-1)]
    tab = rel_table.astype(F32)
    tab = (tab - tab[NUM_BUCKETS - 1:NUM_BUCKETS]) * LOG2E
    vals = jnp.dot(jnp.asarray(onehot), tab, precision=lax.Precision.HIGHEST)
    vals = jnp.where(jnp.asarray(dist.reshape(-1, 1) >= 0), vals, NEG_INF)
    return jnp.moveaxis(vals.reshape(dist.shape + (H_B,)), -1, 0)


def _toeplitz(w, n):
    length = w.shape[1]
    a = jnp.broadcast_to(w[:, None, :], (w.shape[0], n, length))
    a = jnp.pad(a, ((0, 0), (0, 0), (0, 1))).reshape(w.shape[0], n * (length + 1))
    return a[:, :n * length].reshape(w.shape[0], n, length)


def _attn_prompt_kernel(lq_ref, g_ref, qt_ref, kb_ref, vt_ref, bias_ref, o_ref,
                        qcat_ref, m_ref, acc_ref, sa_ref, sb_ref, sd_ref, *, hps, lam_init):
    tq = qt_ref.shape[3]
    lax.fori_loop(0, qt_ref.shape[1], functools.partial(
        _attn_prompt_block, lq_ref, g_ref, qt_ref, kb_ref, vt_ref, bias_ref, o_ref,
        qcat_ref, m_ref, acc_ref, sa_ref, sb_ref, sd_ref, hps, lam_init, tq), 0)


def _attn_prompt_block(lq_ref, g_ref, qt_ref, kb_ref, vt_ref, bias_ref, o_ref,
                       qcat_ref, m_ref, acc_ref, sa_ref, sb_ref, sd_ref, hps, lam_init, tq, qi, carry):
    ones = jnp.ones((V7X_BF16_ROWS, tq), BF16)
    zero = jnp.zeros((DH, tq), BF16)

    for h in range(hps):
        qt = qt_ref[0, qi, h * DV:(h + 1) * DV, :]
        qcat_ref[h] = jnp.concatenate([jnp.concatenate([qt[:DH], zero], axis=0),
                                       jnp.concatenate([zero, qt[DH:]], axis=0)], axis=1)
        m_ref[h] = jnp.full(m_ref.shape[1:], NEG_INF, F32)
        acc_ref[h] = jnp.zeros(acc_ref.shape[1:], F32)

    def scores(j, s_ref):
        ks = pl.multiple_of(j * tq, tq)
        for h in range(hps):
            s_ref[h] = _dot(kb_ref[0, pl.ds(ks, tq), h * DV:(h + 1) * DV], qcat_ref[h])

    def update(j, s_ref, kind):
        for h in range(hps):
            s = s_ref[h]
            if kind is not None:
                b = bias_ref[h, kind]
                s = s + jnp.concatenate([b, b], axis=1)
            m_old = m_ref[h]
            m_new = jnp.maximum(m_old, jnp.max(s, axis=0, keepdims=True))
            alpha = jnp.exp2(m_old - m_new)
            p = jnp.exp2(s - m_new).astype(BF16)
            vext = jnp.concatenate([vt_ref[0, j, h * DV:(h + 1) * DV, :], ones], axis=0)
            acc_ref[h] = alpha * acc_ref[h] + _dot(vext, p)
            m_ref[h] = m_new

    n_far = jnp.maximum(qi - 1, 0)
    n_pair = n_far // 2
    scores(0, sa_ref)

    def far_pair(i, carry):
        j = 2 * i
        scores(j + 1, sb_ref)
        update(j, sa_ref, None)
        scores(j + 2, sa_ref)
        update(j + 1, sb_ref, None)
        return carry

    lax.fori_loop(0, n_pair, far_pair, 0)

    @pl.when(n_far % 2 == 1)
    def _():
        scores(qi - 1, sb_ref)
        update(qi - 2, sa_ref, None)
        scores(qi, sd_ref)
        update(qi - 1, sb_ref, 0)

    @pl.when((n_far % 2 == 0) & (qi >= 1))
    def _():
        scores(qi, sd_ref)
        update(qi - 1, sa_ref, 0)

    @pl.when(qi == 0)
    def _():
        scores(0, sd_ref)

    update(qi, sd_ref, 1)

    lam = _lam(lq_ref, lam_init)
    rows = pl.ds(pl.multiple_of(qi * tq, tq), tq)
    for h in range(hps):
        acc = acc_ref[h]
        attn = acc[:DV] * (1.0 / acc[DV:DV + 1])
        o = (attn[:, :tq] - lam * attn[:, tq:]).T
        o_ref[0, rows, h * DV:(h + 1) * DV] = _subln(o, g_ref[...], lam_init).astype(o_ref.dtype)
    return carry


def _attn_prompt(qt, kb, vt, rel_table, lq, subln_g, lam_init):
    bsz, nblk, _, tq = qt.shape
    t = nblk * tq
    hps = HEADS_PER_STEP
    w = _bias_of_distance(rel_table, np.arange(3 * tq) - tq)
    skew = _toeplitz(w, tq)
    bias = jnp.stack([skew[:, :, 2 * tq:], skew[:, :, tq:2 * tq]], axis=1)
    kernel = functools.partial(_attn_prompt_kernel, hps=hps, lam_init=lam_init)
    return pl.pallas_call(
        kernel,
        grid=(bsz, H_B // hps),
        in_specs=[
            _const_spec((4, DH)),
            _const_spec((1, DV)),
            pl.BlockSpec((1, nblk, hps * DV, tq), lambda b, g: (b, 0, g, 0)),
            pl.BlockSpec((1, t, hps * DV), lambda b, g: (b, 0, g)),
            pl.BlockSpec((1, nblk, hps * DV, tq), lambda b, g: (b, 0, g, 0)),
            pl.BlockSpec((hps, 2, tq, tq), lambda b, g: (g, 0, 0, 0)),
        ],
        out_specs=pl.BlockSpec((1, t, hps * DV), lambda b, g: (b, 0, g)),
        out_shape=jax.ShapeDtypeStruct((bsz, t, E_B), BF16),
        scratch_shapes=[
            pltpu.VMEM((hps, DV, 2 * tq), BF16),
            pltpu.VMEM((hps, 1, 2 * tq), F32),
            pltpu.VMEM((hps, DV + V7X_BF16_ROWS, 2 * tq), F32),
            pltpu.VMEM((hps, tq, 2 * tq), F32),
            pltpu.VMEM((hps, tq, 2 * tq), F32),
            pltpu.VMEM((hps, tq, 2 * tq), F32),
        ],
        compiler_params=pltpu.CompilerParams(
            dimension_semantics=("arbitrary", "arbitrary"),
            vmem_limit_bytes=V7X_VMEM_LIMIT_BYTES),
        name="attn_prompt",
    )(lq, subln_g.reshape(1, DV), qt, kb, vt, bias)


class _Stream(NamedTuple):
    n_seq: int
    n_groups: int
    ppg: int
    nbuf: int
    base: int
    lam_init: float


class _StreamRefs(NamedTuple):
    pt: object
    lq: object
    g: object
    q: object
    kn: object
    vn: object
    bias: object
    ck: object
    cv: object
    o: object
    qbd: object
    kbuf: object
    vbuf: object
    kall: object
    vall: object
    m: object
    l: object
    acc: object
    sem: object


def _stream_copies(cfg, r, lin, slot):
    seq = lax.div(lin, cfg.n_groups)
    first_page = lax.rem(lin, cfg.n_groups) * cfg.ppg
    copies = []
    for i in range(cfg.ppg):
        page = cfg.base + r.pt[seq, first_page + i]
        copies.append(pltpu.make_async_copy(r.ck.at[page], r.kbuf.at[slot, i], r.sem.at[0, slot]))
        copies.append(pltpu.make_async_copy(r.cv.at[page], r.vbuf.at[slot, i], r.sem.at[1, slot]))
    return copies


def _start_all(copies):
    for n, cp in enumerate(copies):
        cp.start(priority=n % 2)


def _stream_init(r):
    ncol, tdec = r.qbd.shape[0], r.q.shape[1]
    r.m[...] = jnp.full(r.m.shape, NEG_INF, F32)
    r.l[...] = jnp.zeros(r.l.shape, F32)
    r.acc[...] = jnp.zeros(r.acc.shape, F32)
    qrep = jnp.concatenate([r.q[0]] * (ncol // tdec), axis=0)
    rows = lax.broadcasted_iota(jnp.int32, (ncol, DV), 0)
    head, cmap = rows // (2 * tdec), (rows // tdec) % 2
    fmap = lax.broadcasted_iota(jnp.int32, (ncol, DV), 1) // DH
    blocks = []
    for p in range(HEAD_PAIRS):
        own = jnp.where(head < HEAD_PAIRS, qrep[:, p * DV:(p + 1) * DV],
                        qrep[:, (p + HEAD_PAIRS) * DV:(p + HEAD_PAIRS + 1) * DV])
        blocks.append(jnp.where((head % HEAD_PAIRS == p) & (fmap == cmap), own, 0.0))
    r.qbd[...] = jnp.concatenate(blocks, axis=1).astype(BF16)


def _stream_update(r, kb, vb, tail_bias):
    rows_per_head = r.qbd.shape[0] // H_B
    s = _dot_nt(r.qbd[...], kb)
    n_slab = s.shape[1] // VKEYS
    slabs = [s[:, j * VKEYS:(j + 1) * VKEYS] + (tail_bias if tail_bias is not None and j == n_slab - 1 else r.bias[2])
             for j in range(n_slab)]
    s = slabs[0] if n_slab == 1 else jnp.concatenate(slabs, axis=1)
    m_old = r.m[...]
    m_new = jnp.maximum(m_old, jnp.max(s, axis=1, keepdims=True))
    alpha = jnp.exp2(m_old - m_new)
    p = jnp.exp2(s - m_new)
    r.l[...] = alpha * r.l[...] + jnp.sum(p, axis=1, keepdims=True)
    pv = _dot(p.astype(BF16), vb)
    for h in range(H_B):
        rows = slice(h * rows_per_head, (h + 1) * rows_per_head)
        own = h % HEAD_PAIRS
        r.acc[rows, :] = alpha[rows] * r.acc[rows, :] + pv[rows, own * DV:(own + 1) * DV]
    r.m[...] = m_new


def _stream_group_pieces(cfg, r, lin, slot, tail_bias):
    def fetch():
        ahead = lin + (cfg.nbuf - 1)

        @pl.when(ahead < cfg.n_seq * cfg.n_groups)
        def _():
            _start_all(_stream_copies(cfg, r, ahead, (slot + cfg.nbuf - 1) % cfg.nbuf))

        for cp in _stream_copies(cfg, r, lin, slot):
            cp.wait()

    def gather(i):
        for p in range(HEAD_PAIRS):
            dst = (slice(i * VKEYS, (i + 1) * VKEYS), slice(p * DV, (p + 1) * DV))
            r.kall[dst] = r.kbuf[slot, i, pl.ds(p, VKEYS, stride=HEAD_PAIRS), :].astype(BF16)
            r.vall[dst] = r.vbuf[slot, i, pl.ds(p, VKEYS, stride=HEAD_PAIRS), :].astype(BF16)

    def fold():
        _stream_update(r, r.kall[...], r.vall[...], tail_bias)

    return [fetch] + [functools.partial(gather, i) for i in range(cfg.ppg)] + [fold]


def _stream_finish(cfg, r):
    tdec = r.q.shape[1]
    rows_per_head = r.qbd.shape[0] // H_B
    pad = jnp.zeros((VKEYS - r.kn.shape[1], HEAD_PAIRS * DV), F32)
    _stream_update(r, jnp.concatenate([r.kn[0], pad], axis=0).astype(BF16),
                   jnp.concatenate([r.vn[0], pad], axis=0).astype(BF16), r.bias[1])
    attn = r.acc[...] * (1.0 / r.l[...])
    lam = _lam(r.lq, cfg.lam_init)
    for h in range(H_B):
        r0 = h * rows_per_head
        o = attn[r0:r0 + tdec] - lam * attn[r0 + tdec:r0 + 2 * tdec]
        r.o[0, :, h * DV:(h + 1) * DV] = _subln(o, r.g[...], cfg.lam_init).astype(r.o.dtype)


def _stream_bias(rel_table, tdec):
    tpos = np.arange(tdec)[:, None]
    key = np.arange(PAGE_SIZE)[None, :]
    last = _bias_of_distance(rel_table, PAGE_SIZE + tpos - key)
    self_ = _bias_of_distance(rel_table, np.where(key < tdec, tpos - key, -1))
    bias = jnp.stack([last, self_, jnp.zeros_like(last)])
    ncol = H_B * 2 * tdec
    bias = jnp.broadcast_to(bias[:, :, None], (3, H_B, 2, tdec, PAGE_SIZE)).reshape(3, ncol, PAGE_SIZE)
    own_half = (np.arange(ncol)[:, None] // (2 * tdec)) // HEAD_PAIRS == np.arange(VKEYS)[None, :] % 2
    return jnp.where(jnp.asarray(own_half), jnp.repeat(bias, 2, axis=2), NEG_INF)


def _attn_sample_kernel(pt_ref, *refs, cfg):
    r = _StreamRefs(pt_ref, *refs)
    b = pl.program_id(0)

    @pl.when(b == 0)
    def _():
        for g in range(cfg.nbuf - 1):
            _start_all(_stream_copies(cfg, r, g, g))

    _stream_init(r)

    def ring_round(it, carry):
        for slot in range(cfg.nbuf):
            g = it * cfg.nbuf + slot
            tail_bias = jnp.where(g == cfg.n_groups - 1, r.bias[0], r.bias[2]) if slot == cfg.nbuf - 1 else None
            for piece in _stream_group_pieces(cfg, r, b * cfg.n_groups + g, slot, tail_bias):
                piece()
        return carry

    lax.fori_loop(0, cfg.n_groups // cfg.nbuf, ring_round, 0)
    _stream_finish(cfg, r)


def _attn_sample(q, kn, vn, cache_k, cache_v, page_table, layer, rel_table, lq, subln_g, lam_init):
    bsz, tdec, _ = q.shape
    n_pages = page_table.shape[1]
    n_pool = cache_k.shape[1]
    assert n_pages % (PAGES_PER_GROUP * RING_SLOTS) == 0
    ncol = H_B * 2 * tdec
    assert ncol == V7X_LANES, "score rows (head, map, token) must fill one lane tile after P.V"
    cfg = _Stream(n_seq=bsz, n_groups=n_pages // PAGES_PER_GROUP, ppg=PAGES_PER_GROUP, nbuf=RING_SLOTS,
                  base=layer * n_pool, lam_init=lam_init)
    bias = _stream_bias(rel_table, tdec)
    ck = cache_k.reshape(cache_k.shape[0] * n_pool, PAGE_SIZE * H_B, DV)
    cv = cache_v.reshape(cache_v.shape[0] * n_pool, PAGE_SIZE * H_B, DV)

    per_seq = lambda b, pt: (b, 0, 0)
    ring = (cfg.nbuf, cfg.ppg, PAGE_SIZE * H_B, DV)
    grid_spec = pltpu.PrefetchScalarGridSpec(
        num_scalar_prefetch=1,
        grid=(bsz,),
        in_specs=[
            pl.BlockSpec((4, DH), lambda b, pt: (0, 0)),
            pl.BlockSpec((1, DV), lambda b, pt: (0, 0)),
            pl.BlockSpec((1, tdec, E_B), per_seq),
            pl.BlockSpec((1, 2 * tdec, HEAD_PAIRS * DV), per_seq),
            pl.BlockSpec((1, 2 * tdec, HEAD_PAIRS * DV), per_seq),
            pl.BlockSpec((3, ncol, VKEYS), lambda b, pt: (0, 0, 0)),
            pl.BlockSpec(memory_space=pl.ANY),
            pl.BlockSpec(memory_space=pl.ANY),
        ],
        out_specs=pl.BlockSpec((1, tdec, E_B), per_seq),
        scratch_shapes=[
            pltpu.VMEM((ncol, HEAD_PAIRS * DV), BF16),
            pltpu.VMEM(ring, F32),
            pltpu.VMEM(ring, F32),
            pltpu.VMEM((cfg.ppg * VKEYS, HEAD_PAIRS * DV), BF16),
            pltpu.VMEM((cfg.ppg * VKEYS, HEAD_PAIRS * DV), BF16),
            pltpu.VMEM((ncol, 1), F32),
            pltpu.VMEM((ncol, 1), F32),
            pltpu.VMEM((ncol, DV), F32),
            pltpu.SemaphoreType.DMA((2, cfg.nbuf)),
        ],
    )
    return pl.pallas_call(
        functools.partial(_attn_sample_kernel, cfg=cfg),
        grid_spec=grid_spec,
        out_shape=jax.ShapeDtypeStruct((bsz, tdec, E_B), BF16),
        compiler_params=pltpu.CompilerParams(
            dimension_semantics=("arbitrary",), vmem_limit_bytes=V7X_VMEM_LIMIT_BYTES),
        name="attn_sample",
    )(page_table, lq, subln_g.reshape(1, DV), q, kn.reshape(bsz, 2 * tdec, HEAD_PAIRS * DV),
      vn.reshape(bsz, 2 * tdec, HEAD_PAIRS * DV), bias, ck, cv)


def _post_kernel(x_ref, attn_ref, p_ref, w_u, w_v, w_za, w_zb, w_ga, w_gb, b_gate, gln_g, gln_b,
                 wmix, bmix, w_pa, w_pb, w_o, ln_g, ln_b, w_pe, w_pg, b_pg, *outs, alpha):
    y_ref = outs[0]
    x = x_ref[...]
    xb = x.astype(BF16)
    tm = x.shape[0]
    ck = wmix.shape[1]

    vn = _layer_norm(_gelu(_dot(xb, w_v[...])), gln_g[...], gln_b[...])
    if len(outs) > 1:
        outs[1][...] = vn
    vnb = vn.astype(BF16)
    chunks = []
    for c in range(tm // ck):
        rows = slice(c * ck, (c + 1) * ck)
        groups = [_dot(wmix[g], vnb[rows, g * C_A:(g + 1) * C_A]) + bmix[g] for g in range(G_A)]
        chunks.append(jnp.concatenate(groups, axis=1))
    mixed = jnp.concatenate(chunks, axis=0) if len(chunks) > 1 else chunks[0]
    out_a = _gelu(_dot(xb, w_u[...])) * mixed * _silu(_dot(xb, w_za[...]))
    out_b = attn_ref[...].astype(F32) * _silu(_dot(xb, w_zb[...]))
    gate_a = jax.nn.sigmoid(_dot(xb, w_ga[...]) + b_gate[:, :D_MODEL])
    gate_b = jax.nn.sigmoid(_dot(xb, w_gb[...]) + b_gate[:, D_MODEL:])
    merged = (gate_a * _dot(out_a.astype(BF16), w_pa[...])
              + gate_b * _dot(out_b.astype(BF16), w_pb[...]))
    x1 = _layer_norm(alpha * x + _dot(merged.astype(BF16), w_o[...]), ln_g[...], ln_b[...])
    emb = _dot(p_ref[...].astype(BF16), w_pe[...])
    y_ref[...] = x1 + jax.nn.sigmoid(_dot(x1.astype(BF16), w_pg[...]) + b_pg[...]) * emb


def _post(x2d, attn2d, p2d, w_in, weights, wmix, bmix, alpha, want_v_rows):
    n = x2d.shape[0]
    tm = min(ROW_TILE, n)
    assert n % tm == 0 and tm % wmix.shape[1] == 0
    row = lambda i: (i, 0)
    in_cols = [0, E_A, 2 * E_A, 3 * E_A + 3 * E_B, 3 * E_A + 4 * E_B, 3 * E_A + 4 * E_B + D_MODEL]
    consts = list(weights[:3]) + [wmix, bmix] + list(weights[3:])
    out_shape = [jax.ShapeDtypeStruct((n, D_MODEL), F32)]
    out_specs = [pl.BlockSpec((tm, D_MODEL), row)]
    if want_v_rows:
        out_shape.append(jax.ShapeDtypeStruct((n, E_A), F32))
        out_specs.append(pl.BlockSpec((tm, E_A), row))
    res = pl.pallas_call(
        functools.partial(_post_kernel, alpha=alpha),
        grid=(n // tm,),
        in_specs=[pl.BlockSpec((tm, D_MODEL), row), pl.BlockSpec((tm, E_B), row),
                  pl.BlockSpec((tm, P_DIM), row)]
                 + [_cols_spec(D_MODEL, c, D_MODEL) for c in in_cols] + [_const_spec(c.shape) for c in consts],
        out_specs=out_specs,
        out_shape=out_shape,
        compiler_params=pltpu.CompilerParams(
            dimension_semantics=("arbitrary",), vmem_limit_bytes=V7X_VMEM_LIMIT_BYTES),
        name="post",
    )(x2d, attn2d, p2d, *([w_in] * len(in_cols)), *consts)
    return res if want_v_rows else (res[0], None)


def kernel(x_prompt, x_sample, p_prompt, p_sample, cache_k, cache_v, page_table, rel_table, w_in, b_gate, gmlp_ln_g, gmlp_ln_b, w_s, b_s, lambda_qk, subln_g, w_pa, w_pb, w_o, ln_g, ln_b, w_pe, w_pg, b_pg):
    depth = w_in.shape[0]
    alpha = (2.0 * depth) ** 0.25
    bsz, seq, _ = x_prompt.shape
    dbsz, dseq, _ = x_sample.shape
    chunk = w_s.shape[-1]
    n_seq_tile = min(ROW_TILE, dbsz * dseq) // dseq
    nblk = seq // Q_TILE

    y_p = x_prompt.reshape(bsz * seq, D_MODEL)
    y_s = x_sample.reshape(dbsz * dseq, D_MODEL)
    kp_rows, vp_rows, ks_rows, vs_rows, gs_rows = [], [], [], [], []
    for l in range(depth):
        lam_init = _lambda_init(l)
        wl = w_in[l].astype(BF16)
        row2 = lambda a: a.reshape(1, -1).astype(F32)
        weights = (row2(b_gate[l]), row2(gmlp_ln_g[l]), row2(gmlp_ln_b[l]),
                   w_pa[l].astype(BF16), w_pb[l].astype(BF16), w_o[l].astype(BF16),
                   row2(ln_g[l]), row2(ln_b[l]), w_pe[l].astype(BF16), w_pg[l].astype(BF16), row2(b_pg[l]))

        tril_p = jnp.tril(jnp.ones((chunk, chunk), bool))
        wmix_p = jnp.where(tril_p, w_s[l], 0.0).astype(BF16)
        bmix_p = jnp.broadcast_to(b_s[l][:, :, None], (G_A, chunk, C_A)).astype(F32)
        ws_d = jnp.where(jnp.tril(jnp.ones((dseq, dseq), bool)), w_s[l][:, :dseq, :dseq], 0.0)
        rep = jnp.asarray(np.tile(np.eye(dseq, dtype=np.float32), (n_seq_tile, 1)))
        same_seq = np.kron(np.eye(n_seq_tile), np.ones((dseq, dseq))) > 0
        tiled = jnp.einsum('ia,gab,jb->gij', rep, ws_d, rep, precision=lax.Precision.HIGHEST)
        wmix_s = jnp.where(jnp.asarray(same_seq), tiled, 0.0).astype(BF16)
        bmix_s = jnp.broadcast_to(jnp.tile(b_s[l][:, :dseq], (1, n_seq_tile))[:, :, None],
                                  (G_A, n_seq_tile * dseq, C_A)).astype(F32)

        lq = lambda_qk[l].astype(F32)

        qt, kb, vt, k, v = _qkv_proj(y_p, wl, True)
        attn = _attn_prompt(qt.reshape(bsz, nblk, E_B, Q_TILE), kb.reshape(bsz, seq, E_B),
                            vt.reshape(bsz, nblk, E_B, Q_TILE), rel_table, lq, subln_g[l], lam_init)
        y_p, _ = _post(y_p, attn.reshape(bsz * seq, E_B), p_prompt[l].reshape(bsz * seq, P_DIM),
                       wl, weights, wmix_p, bmix_p, alpha, False)
        kp_rows.append(k.reshape(bsz, seq, H_B, DV))
        vp_rows.append(v.reshape(bsz, seq, H_B, DV))

        q, kn, vn, k, v = _qkv_proj(y_s, wl, False)
        attn = _attn_sample(q.reshape(dbsz, dseq, E_B), kn.reshape(dbsz, dseq, E_B), vn.reshape(dbsz, dseq, E_B),
                            cache_k, cache_v, page_table, l, rel_table, lq, subln_g[l], lam_init)
        y_s, g_rows = _post(y_s, attn.reshape(dbsz * dseq, E_B), p_sample[l].reshape(dbsz * dseq, P_DIM),
                            wl, weights, wmix_s, bmix_s, alpha, True)
        ks_rows.append(k.reshape(dbsz, dseq, H_B, DV))
        vs_rows.append(v.reshape(dbsz, dseq, H_B, DV))
        gs_rows.append(g_rows.reshape(dbsz, dseq, E_A))

    return (y_p.reshape(bsz, seq, D_MODEL), y_s.reshape(dbsz, dseq, D_MODEL),
            jnp.stack(kp_rows), jnp.stack(vp_rows), jnp.stack(ks_rows), jnp.stack(vs_rows),
            jnp.stack(gs_rows))
```
